```python
import jax, jax.numpy as jnp
from jax import lax
import numpy as np

D_MODEL = 1024
BATCH = 8
SEQ = 4096
DEPTH = 2

CHUNK = 64
PLE_DIM = 256
N_EVEN = (DEPTH + 1) // 2
N_ODD = DEPTH // 2
CONV_WIDTH = 4
NORM_EPS = 1e-6

RWKV_WIDTH = D_MODEL
RWKV_HEAD_DIM = 64
RWKV_HEADS = RWKV_WIDTH // RWKV_HEAD_DIM
DECAY_RANK = 64
ICL_RANK = 64
DECAY_SCALE = 0.6065306597126334
RWKV_GN_EPS = 64e-5

MLSTM_WIDTH = D_MODEL
MLSTM_HEADS = 4
MLSTM_HEAD_DIM = MLSTM_WIDTH // MLSTM_HEADS
QKV_BLOCK = 4
MLSTM_EPS = 1e-6
MLSTM_LN_EPS = 1e-5

AB_SPLIT_SIZES = (RWKV_WIDTH, RWKV_WIDTH, RWKV_WIDTH, DECAY_RANK, ICL_RANK, MLSTM_WIDTH, RWKV_WIDTH, MLSTM_WIDTH)
AB_IN_WIDTH = 4 * RWKV_WIDTH + DECAY_RANK + ICL_RANK + 2 * MLSTM_WIDTH
AB_OUT_WIDTH = RWKV_WIDTH + MLSTM_WIDTH

LRU_WIDTH = 2 * D_MODEL
LRU_BLOCKS = 16
LRU_C = 8.0

kernel_name = 'rwkv7_mlstm_rglru_hybrid_trunk'


def rmsnorm(x, g):
    xf = x.astype(jnp.float32)
    y = xf * lax.rsqrt(jnp.mean(xf * xf, axis=-1, keepdims=True) + NORM_EPS)
    return (y * g.astype(jnp.float32)).astype(x.dtype)


def head_layernorm(x, eps):
    mu = jnp.mean(x, axis=-1, keepdims=True)
    var = jnp.mean(jnp.square(x - mu), axis=-1, keepdims=True)
    return (x - mu) * lax.rsqrt(var + eps)


def causal_conv(x, w, b):
    c = x.shape[-1]
    y = lax.conv_general_dilated(x, w[:, None, :].astype(x.dtype), window_strides=(1,),
                                 padding=[(CONV_WIDTH - 1, 0)],
                                 dimension_numbers=('NWC', 'WIO', 'NWC'),
                                 feature_group_count=c)
    return y + b


def token_shift_mix(u, mu):
    prev = jnp.pad(u, ((0, 0), (1, 0), (0, 0)))[:, :-1]
    return u + (prev - u) * mu


def block_diag_linear(x, w):
    nb, bi, bo = w.shape
    xs = x.reshape(x.shape[:-1] + (nb, bi))
    return jnp.einsum('...gi,gio->...go', xs, w).reshape(x.shape[:-1] + (nb * bo,))


def rwkv7_recurrence(r, w, k, v, kk, a):
    b_, _, h_, n_ = r.shape
    xs = tuple(jnp.moveaxis(t, 1, 0) for t in (r, w, k, v, kk, a))

    def step(state, inp):
        r_t, w_t, k_t, v_t, kk_t, a_t = inp
        sa = jnp.einsum('bhvk,bhk->bhv', state, kk_t)
        state = (state * w_t[:, :, None, :]
                 - sa[..., None] * (kk_t * a_t)[:, :, None, :]
                 + v_t[..., None] * k_t[:, :, None, :])
        y_t = jnp.einsum('bhvk,bhk->bhv', state, r_t)
        return state, y_t

    s0 = jnp.zeros((b_, h_, n_, n_), jnp.float32)
    _, y = lax.scan(step, s0, xs)
    return jnp.moveaxis(y, 0, 1)


def mlstm_chunkwise(q, k, v, li, lf):
    b_, s_, h_, d_ = q.shape
    nc = s_ // CHUNK

    def to_chunks(t):
        return t.reshape(b_, nc, CHUNK, h_, d_).transpose(1, 0, 3, 2, 4)

    def gate_chunks(t):
        return t.reshape(b_, nc, CHUNK, h_).transpose(1, 0, 3, 2)

    causal = jnp.tril(jnp.ones((CHUNK, CHUNK), dtype=bool))

    def step(carry, inp):
        c_mat, n_vec, m = carry
        q_c, k_c, v_c, li_c, lf_c = inp
        bcum = jnp.cumsum(lf_c, axis=-1)
        log_d = jnp.where(causal, bcum[..., :, None] - bcum[..., None, :] + li_c[..., None, :], -jnp.inf)
        log_inter = bcum + m[..., None]
        m_t = jnp.maximum(jnp.max(log_d, axis=-1), log_inter)
        s = jnp.einsum('bhtd,bhsd->bhts', q_c, k_c) * jnp.exp(log_d - m_t[..., None])
        w_inter = jnp.exp(log_inter - m_t)
        num = (jnp.einsum('bhts,bhsd->bhtd', s, v_c)
               + w_inter[..., None] * jnp.einsum('bhvk,bhtk->bhtv', c_mat, q_c))
        den = jnp.sum(s, axis=-1) + w_inter * jnp.einsum('bhk,bhtk->bht', n_vec, q_c)
        h = num / (jnp.maximum(jnp.abs(den), jnp.exp(-m_t)) + MLSTM_EPS)[..., None]
        b_last = bcum[..., -1]
        log_g = b_last[..., None] - bcum + li_c
        m_new = jnp.maximum(b_last + m, jnp.max(log_g, axis=-1))
        g = jnp.exp(log_g - m_new[..., None])
        decay = jnp.exp(b_last + m - m_new)
        c_mat = decay[..., None, None] * c_mat + jnp.einsum('bhsv,bhsk->bhvk', g[..., None] * v_c, k_c)
        n_vec = decay[..., None] * n_vec + jnp.einsum('bhs,bhsk->bhk', g, k_c)
        return (c_mat, n_vec, m_new), h

    init = (jnp.zeros((b_, h_, d_, d_), jnp.float32),
            jnp.zeros((b_, h_, d_), jnp.float32),
            jnp.zeros((b_, h_), jnp.float32))
    xs = (to_chunks(q), to_chunks(k), to_chunks(v), gate_chunks(li), gate_chunks(lf))
    _, h = lax.scan(step, init, xs)
    return h.transpose(1, 0, 3, 2, 4).reshape(b_, s_, h_, d_)


def rwkv_mlstm_layer(xn, w_in, mu, mu_lora, w0, w_up, a0, a_up, k_k, k_a, r_k, ln_w, ln_b,
                     conv_w, conv_b, wq, wk, wv, w_if, b_if, m_norm, m_skip, w_out):
    f32 = jnp.float32
    b_, s_, _ = xn.shape
    z = xn @ w_in
    split_at = np.cumsum(AB_SPLIT_SIZES)[:-1].tolist()
    zr, zk, zv, zwd, zad, xm, gr, gm = jnp.split(z, split_at, axis=-1)

    r = token_shift_mix(zr, mu[0])
    k = token_shift_mix(zk, mu[1])
    v = token_shift_mix(zv, mu[2])
    wd = token_shift_mix(zwd, mu_lora[0])
    ad = token_shift_mix(zad, mu_lora[1])
    log_w = -DECAY_SCALE * jax.nn.sigmoid((w0 + jnp.tanh(wd) @ w_up).astype(f32))
    a = jax.nn.sigmoid((a0 + ad @ a_up).astype(f32))

    def heads(t):
        return t.reshape(b_, s_, RWKV_HEADS, RWKV_HEAD_DIM)

    kk = heads((k * k_k).astype(f32))
    kk = kk / jnp.maximum(jnp.sqrt(jnp.sum(kk * kk, axis=-1, keepdims=True)), 1e-12)
    k_eff = heads(k.astype(f32) * (1.0 + (a - 1.0) * k_a.astype(f32)))
    rh = heads(r.astype(f32))
    vh = heads(v.astype(f32))
    y = rwkv7_recurrence(rh, heads(jnp.exp(log_w)), k_eff, vh, kk, heads(a))
    y = (head_layernorm(y, RWKV_GN_EPS) * ln_w.reshape(RWKV_HEADS, RWKV_HEAD_DIM)
         + ln_b.reshape(RWKV_HEADS, RWKV_HEAD_DIM))
    y = y + jnp.sum(rh * k_eff * r_k, axis=-1, keepdims=True) * vh
    y_rwkv = y.reshape(b_, s_, RWKV_WIDTH).astype(xn.dtype)

    xc = jax.nn.silu(causal_conv(xm, conv_w, conv_b))
    q = block_diag_linear(xc, wq)
    km = block_diag_linear(xc, wk)
    vm = block_diag_linear(xm, wv)
    gates = (jnp.concatenate([q, km, vm], axis=-1) @ w_if + b_if).astype(f32)
    li = gates[..., :MLSTM_HEADS]
    lf = jax.nn.log_sigmoid(gates[..., MLSTM_HEADS:])

    def mheads(t):
        return t.reshape(b_, s_, MLSTM_HEADS, MLSTM_HEAD_DIM).astype(f32)

    h = mlstm_chunkwise(mheads(q), mheads(km) * (MLSTM_HEAD_DIM ** -0.5), mheads(vm), li, lf)
    h = head_layernorm(h, MLSTM_LN_EPS).reshape(b_, s_, MLSTM_WIDTH) * m_norm
    y_m = (h + m_skip * xc.astype(f32)).astype(xn.dtype)

    merged = jnp.concatenate([y_rwkv * jax.nn.silu(gr), y_m * jax.nn.silu(gm)], axis=-1)
    return merged @ w_out


def rglru_layer(xn, w_in, conv_w, conv_b, wr, br, wi, bi, lam, w_out):
    f32 = jnp.float32
    z = xn @ w_in
    xl, g = jnp.split(z, [LRU_WIDTH], axis=-1)
    xc = causal_conv(xl, conv_w, conv_b)
    rg = jax.nn.sigmoid((block_diag_linear(xc, wr) + br).astype(f32))
    ig = jax.nn.sigmoid((block_diag_linear(xc, wi) + bi).astype(f32))
    log_a = -LRU_C * rg * jax.nn.softplus(-lam.astype(f32))
    a = jnp.exp(log_a)
    mult = jnp.sqrt(jnp.maximum(-jnp.expm1(2.0 * log_a), 0.0))
    u = xc.astype(f32) * ig * mult

    def combine(c1, c2):
        a1, b1 = c1
        a2, b2 = c2
        return a1 * a2, a2 * b1 + b2

    _, h = lax.associative_scan(combine, (a, u), axis=1)
    return (h.astype(xn.dtype) * jax.nn.silu(g)) @ w_out


def setup_inputs(seed: int = 0) -> dict:
    key = jax.random.key(seed)
    keys = iter(jax.random.split(key, 48))

    def nrm(shape, scale):
        return scale * jax.random.normal(next(keys), shape, jnp.float32)

    def unif(shape, lo, hi):
        return jax.random.uniform(next(keys), shape, jnp.float32, minval=lo, maxval=hi)

    x = nrm((BATCH, SEQ, D_MODEL), 1.0)
    p = nrm((DEPTH, BATCH, SEQ, PLE_DIM), 1.0)
    mix_norm = 1.0 + nrm((DEPTH, D_MODEL), 0.02)
    pe_norm = 1.0 + nrm((DEPTH, D_MODEL), 0.02)
    final_norm = 1.0 + nrm((D_MODEL,), 0.02)
    pe_up = nrm((DEPTH, PLE_DIM, D_MODEL), PLE_DIM ** -0.5)
    pe_gate = nrm((DEPTH, D_MODEL, D_MODEL), D_MODEL ** -0.5)

    ab_w_in = nrm((N_EVEN, D_MODEL, AB_IN_WIDTH), D_MODEL ** -0.5)
    rwkv_mu = unif((N_EVEN, 3, RWKV_WIDTH), 0.0, 1.0)
    rwkv_mu_lora = unif((N_EVEN, 2, DECAY_RANK), 0.0, 1.0)
    rwkv_w0 = unif((N_EVEN, RWKV_WIDTH), -6.0, 1.0)
    rwkv_w_up = nrm((N_EVEN, DECAY_RANK, RWKV_WIDTH), 0.5 * DECAY_RANK ** -0.5)
    rwkv_a0 = nrm((N_EVEN, RWKV_WIDTH), 0.1)
    rwkv_a_up = nrm((N_EVEN, ICL_RANK, RWKV_WIDTH), 0.5 * ICL_RANK ** -0.5)
    rwkv_k_k = 0.85 + nrm((N_EVEN, RWKV_WIDTH), 0.05)
    rwkv_k_a = 1.0 + nrm((N_EVEN, RWKV_WIDTH), 0.05)
    rwkv_r_k = nrm((N_EVEN, RWKV_HEADS, RWKV_HEAD_DIM), 0.1)
    rwkv_ln_w = 1.0 + nrm((N_EVEN, RWKV_WIDTH), 0.02)
    rwkv_ln_b = nrm((N_EVEN, RWKV_WIDTH), 0.02)

    mlstm_conv_w = nrm((N_EVEN, CONV_WIDTH, MLSTM_WIDTH), 0.5)
    mlstm_conv_b = nrm((N_EVEN, MLSTM_WIDTH), 0.02)
    nblk = MLSTM_WIDTH // QKV_BLOCK
    mlstm_wq = nrm((N_EVEN, nblk, QKV_BLOCK, QKV_BLOCK), QKV_BLOCK ** -0.5)
    mlstm_wk = nrm((N_EVEN, nblk, QKV_BLOCK, QKV_BLOCK), QKV_BLOCK ** -0.5)
    mlstm_wv = nrm((N_EVEN, nblk, QKV_BLOCK, QKV_BLOCK), QKV_BLOCK ** -0.5)
    mlstm_w_if = nrm((N_EVEN, 3 * MLSTM_WIDTH, 2 * MLSTM_HEADS), 0.1 * (3 * MLSTM_WIDTH) ** -0.5)
    i_bias = nrm((N_EVEN, MLSTM_HEADS), 0.1)
    f_bias = jnp.linspace(3.0, 6.0, MLSTM_HEADS, dtype=jnp.float32)[None, :] + nrm((N_EVEN, MLSTM_HEADS), 0.01)
    mlstm_b_if = jnp.concatenate([i_bias, f_bias], axis=-1)
    mlstm_norm = 1.0 + nrm((N_EVEN, MLSTM_WIDTH), 0.02)
    mlstm_skip = 1.0 + nrm((N_EVEN, MLSTM_WIDTH), 0.02)
    ab_w_out = nrm((N_EVEN, AB_OUT_WIDTH, D_MODEL), AB_OUT_WIDTH ** -0.5)

    c_w_in = nrm((N_ODD, D_MODEL, 2 * LRU_WIDTH), D_MODEL ** -0.5)
    c_conv_w = nrm((N_ODD, CONV_WIDTH, LRU_WIDTH), 0.5)
    c_conv_b = nrm((N_ODD, LRU_WIDTH), 0.02)
    bw = LRU_WIDTH // LRU_BLOCKS
    c_wr = nrm((N_ODD, LRU_BLOCKS, bw, bw), bw ** -0.5)
    c_br = nrm((N_ODD, LRU_WIDTH), 0.02)
    c_wi = nrm((N_ODD, LRU_BLOCKS, bw, bw), bw ** -0.5)
    c_bi = nrm((N_ODD, LRU_WIDTH), 0.02)
    a_c = unif((N_ODD, LRU_WIDTH), 0.81, 0.998)
    s_lam = a_c ** (1.0 / LRU_C)
    c_lambda = jnp.log(s_lam) - jnp.log1p(-s_lam)
    c_w_out = nrm((N_ODD, LRU_WIDTH, D_MODEL), LRU_WIDTH ** -0.5)

    return {'x': x, 'p': p, 'mix_norm': mix_norm, 'pe_norm': pe_norm, 'final_norm': final_norm,
            'pe_up': pe_up, 'pe_gate': pe_gate, 'ab_w_in': ab_w_in, 'rwkv_mu': rwkv_mu,
            'rwkv_mu_lora': rwkv_mu_lora, 'rwkv_w0': rwkv_w0, 'rwkv_w_up': rwkv_w_up,
            'rwkv_a0': rwkv_a0, 'rwkv_a_up': rwkv_a_up, 'rwkv_k_k': rwkv_k_k, 'rwkv_k_a': rwkv_k_a,
            'rwkv_r_k': rwkv_r_k, 'rwkv_ln_w': rwkv_ln_w, 'rwkv_ln_b': rwkv_ln_b,
            'mlstm_conv_w': mlstm_conv_w, 'mlstm_conv_b': mlstm_conv_b, 'mlstm_wq': mlstm_wq,
            'mlstm_wk': mlstm_wk, 'mlstm_wv': mlstm_wv, 'mlstm_w_if': mlstm_w_if,
            'mlstm_b_if': mlstm_b_if, 'mlstm_norm': mlstm_norm, 'mlstm_skip': mlstm_skip,
            'ab_w_out': ab_w_out, 'c_w_in': c_w_in, 'c_conv_w': c_conv_w, 'c_conv_b': c_conv_b,
            'c_wr': c_wr, 'c_br': c_br, 'c_wi': c_wi, 'c_bi': c_bi, 'c_lambda': c_lambda,
            'c_w_out': c_w_out}


def reference(x, p, mix_norm, pe_norm, final_norm, pe_up, pe_gate, ab_w_in, rwkv_mu,
              rwkv_mu_lora, rwkv_w0, rwkv_w_up, rwkv_a0, rwkv_a_up, rwkv_k_k, rwkv_k_a,
              rwkv_r_k, rwkv_ln_w, rwkv_ln_b, mlstm_conv_w, mlstm_conv_b, mlstm_wq,
              mlstm_wk, mlstm_wv, mlstm_w_if, mlstm_b_if, mlstm_norm, mlstm_skip,
              ab_w_out, c_w_in, c_conv_w, c_conv_b, c_wr, c_br, c_wi, c_bi, c_lambda,
              c_w_out):
    h = x
    for i in range(DEPTH):
        j = i // 2
        xn = rmsnorm(h, mix_norm[i])
        if i % 2 == 0:
            h = h + rwkv_mlstm_layer(xn, ab_w_in[j], rwkv_mu[j], rwkv_mu_lora[j], rwkv_w0[j],
                                     rwkv_w_up[j], rwkv_a0[j], rwkv_a_up[j], rwkv_k_k[j],
                                     rwkv_k_a[j], rwkv_r_k[j], rwkv_ln_w[j], rwkv_ln_b[j],
                                     mlstm_conv_w[j], mlstm_conv_b[j], mlstm_wq[j], mlstm_wk[j],
                                     mlstm_wv[j], mlstm_w_if[j], mlstm_b_if[j], mlstm_norm[j],
                                     mlstm_skip[j], ab_w_out[j])
        else:
            h = h + rglru_layer(xn, c_w_in[j], c_conv_w[j], c_conv_b[j], c_wr[j], c_br[j],
                                c_wi[j], c_bi[j], c_lambda[j], c_w_out[j])
        hn = rmsnorm(h, pe_norm[i])
        h = h + (p[i] @ pe_up[i]) * jax.nn.sigmoid(hn @ pe_gate[i])
    return rmsnorm(h, final_norm)
```

```python
import functools

import jax
import jax.numpy as jnp
import numpy as np
from jax import lax
from jax.experimental import pallas as pl
from jax.experimental.pallas import tpu as pltpu

F32 = jnp.float32
BF16 = jnp.bfloat16

D = 1024
LANES = 128
CHUNK = 64
NORM_EPS = 1e-6
PLE = 256
RW_HEADS = 16
RW_HD = 64
RW_PAIRS = RW_HEADS // 2
DECAY_SCALE = 0.6065306597126334
RW_GN_EPS = 64e-5
ML_HEADS = 4
ML_HD = 256
ML_EPS = 1e-6
ML_LN_EPS = 1e-5
QKV_BLOCK = 4
CONV_W = 4
LRU_W = 2 * D
LRU_BLK = 128
LRU_C = 8.0
AB_IN = 4 * D + 128 + 2 * D

TM = 256
TS = 512
LRU_STRIP = 512
VMEM_LIMIT = 56 * 1024 * 1024


def _dg(a, b, dims):
    return lax.dot_general(a, b, (dims, ((), ())), preferred_element_type=F32)


def _mm(a, b):
    return _dg(a.astype(BF16), b.astype(BF16), ((1,), (0,)))


def _mm_nt(a, b):
    return _dg(a.astype(BF16), b.astype(BF16), ((1,), (1,)))


def _mm_tn(a, b):
    return _dg(a.astype(BF16), b.astype(BF16), ((0,), (0,)))


def _split2(x):
    hi = x.astype(BF16)
    lo = (x - hi.astype(F32)).astype(BF16)
    return hi, lo


def _split3(x):
    hi = x.astype(BF16)
    r = x - hi.astype(F32)
    mid = r.astype(BF16)
    lo = (r - mid.astype(F32)).astype(BF16)
    return hi, mid, lo


def _mm3(a, b, dims=((1,), (0,))):
    ah, al = _split2(a)
    bh, bl = _split2(b)
    return _dg(ah, bh, dims) + (_dg(ah, bl, dims) + _dg(al, bh, dims))


def _mm_exact_lhs(a_bf16, b, dims=((1,), (0,))):
    h, m, l = _split3(b)
    return _dg(a_bf16, h, dims) + (_dg(a_bf16, m, dims) + _dg(a_bf16, l, dims))


def _mm_exact_rhs(a, b_bf16, dims=((1,), (0,))):
    h, m, l = _split3(a)
    return _dg(h, b_bf16, dims) + (_dg(m, b_bf16, dims) + _dg(l, b_bf16, dims))


def _rms(x, g):
    return x * lax.rsqrt(jnp.mean(x * x, axis=-1, keepdims=True) + NORM_EPS) * g


def _sigmoid(x):
    return 1.0 / (1.0 + jnp.exp(-x))


def _silu(x):
    return x * _sigmoid(x)


def _softplus(x):
    return jnp.maximum(x, 0.0) + jnp.log(1.0 + jnp.exp(-jnp.abs(x)))


def _iota(shape, axis):
    return lax.broadcasted_iota(jnp.int32, shape, axis)


def _tri_incl(n):
    return jnp.where(_iota((n, n), 1) <= _iota((n, n), 0), 1.0, 0.0).astype(BF16)


def _shift_rows(x, d, prev8):
    xs = pltpu.roll(x, d, 0)
    ps = pltpu.roll(prev8, d, 0)
    head = jnp.where(_iota((8, x.shape[1]), 0) < d, ps, xs[:8])
    return jnp.concatenate([head, xs[8:]], axis=0)


def _conv4(x, prev8, w_ref, b_ref):
    acc = x * w_ref[3:4, :] + b_ref[...]
    for d in (1, 2, 3):
        acc = acc + _shift_rows(x, d, prev8) * w_ref[3 - d:4 - d, :]
    return acc


def _block4_linear(x, c_ref):
    n = x.shape[1]
    acc = x * c_ref[3:4, :]
    for d in (-3, -2, -1, 1, 2, 3):
        acc = acc + pltpu.roll(x, d % n, 1) * c_ref[d + 3:d + 4, :]
    return acc


def _const_spec(shape):
    nd = len(shape)
    return pl.BlockSpec(shape, lambda *_: (0,) * nd, pipeline_mode=pl.Buffered(1))


def _params(sem):
    return pltpu.CompilerParams(dimension_semantics=sem, vmem_limit_bytes=VMEM_LIMIT)


def _proj0_body(x_ref, g_ref, w_ref, zr, zk, zv, xm, gr, gm, zl):
    xn = _rms(x_ref[...], g_ref[...]).astype(BF16)
    for j, o in enumerate((zr, zk, zv, xm, gr, gm)):
        o[...] = _dg(xn, w_ref[:, j * D:(j + 1) * D], ((1,), (0,)))
    zl[...] = _dg(xn, w_ref[:, 6 * D:6 * D + 128], ((1,), (0,)))


def _proj0(x2, g, w):
    t = x2.shape[0]
    wide = jax.ShapeDtypeStruct((t, D), F32)
    row = pl.BlockSpec((TM, D), lambda i: (i, 0))
    return pl.pallas_call(
        _proj0_body,
        grid=(t // TM,),
        in_specs=[row, _const_spec((1, D)), _const_spec((D, AB_IN))],
        out_specs=[row] * 6 + [pl.BlockSpec((TM, 128), lambda i: (i, 0))],
        out_shape=[wide] * 6 + [jax.ShapeDtypeStruct((t, 128), F32)],
        compiler_params=_params(("parallel",)),
        name="proj0",
    )(x2, g, w)


_P_MUR, _P_MUK, _P_MUV, _P_W0, _P_A0, _P_KK, _P_KA, _P_RK, _P_LNW, _P_LNB = range(10)


def _rwkv_pair(r, ke, v, lw, g, a, kkraw, rk, lnw, lnb, st, masks):
    m_top, rowhead, strict, incl = masks
    n = CHUNK

    def stack(x):
        return jnp.concatenate([jnp.where(m_top, x, 0.0), jnp.where(m_top, 0.0, x)], axis=0)

    def unstack_col(c):
        return jnp.where(m_top, c[:n], c[n:])

    norm = jnp.sqrt(jnp.sum(jnp.square(stack(kkraw)), axis=-1, keepdims=True))
    kk = kkraw / unstack_col(jnp.maximum(norm, 1e-12))
    bonus = unstack_col(jnp.sum(stack(r * ke * rk), axis=-1, keepdims=True))

    g_last = g[n - 1:n, :]
    b = kk * a
    e_neg = jnp.exp(-g)
    e_tail = jnp.exp(g_last - g)
    rg_s = stack(r * jnp.exp(g))
    ag_s = stack(-kk * jnp.exp(g - lw))
    bi = b * e_neg
    ki = ke * e_neg
    bd_s = stack(b * e_tail)
    kd = ke * e_tail

    sc = _mm3(jnp.concatenate([ag_s, rg_s], axis=0),
              jnp.concatenate([bi, bi, ki, ki], axis=0), ((1,), (1,)))
    n_ab = jnp.where(strict, sc[:2 * n, :2 * n], 0.0)
    n_ak = jnp.where(strict, sc[:2 * n, 2 * n:], 0.0)
    a_rb = jnp.where(incl, sc[2 * n:, :2 * n], 0.0)
    a_rk = jnp.where(incl, sc[2 * n:, 2 * n:], 0.0)

    v2 = jnp.concatenate([v, v], axis=0)
    akv = jnp.where(rowhead, _mm3(n_ak, v2), 0.0)
    arkv = jnp.where(rowhead, _mm3(a_rk, v2), 0.0)

    x = jnp.concatenate([ag_s, akv], axis=1)
    nk = n_ab
    for i in range(6):
        x = x + _mm3(nk, x)
        if i < 5:
            nk = _mm3(nk, nk)
    a_hat = x[:, :LANES]
    w_hat = x[:, LANES:]

    u = _mm3(a_hat, st) + w_hat
    y_s = _mm3(rg_s, st) + _mm3(a_rb, u) + arkv

    e_last_col = jnp.transpose(jnp.broadcast_to(jnp.exp(g_last), (LANES, LANES)))
    kv = jnp.where(rowhead, _mm3(kd, v, ((0,), (0,))), 0.0)
    st_new = e_last_col * st + _mm3(jnp.transpose(bd_s), u) + kv

    mu = jnp.sum(y_s, axis=-1, keepdims=True) * (1.0 / RW_HD)
    dlt = jnp.where(rowhead, y_s - mu, 0.0)
    var = jnp.sum(dlt * dlt, axis=-1, keepdims=True) * (1.0 / RW_HD)
    yn = dlt * lax.rsqrt(var + RW_GN_EPS)
    y = (yn[:n] + yn[n:]) * lnw + lnb + bonus * v
    return y, st_new


def _rwkv_body(zr_ref, zk_ref, zv_ref, zl_ref, gr_ref, pr_ref, mul_ref, wl_ref, o_ref,
               carry, st_ref, cs):
    n = CHUNK

    @pl.when(pl.program_id(1) == 0)
    def _():
        carry[...] = jnp.zeros_like(carry)
        st_ref[...] = jnp.zeros_like(st_ref)

    lane = _iota((1, LANES), 1)
    m_top = lane < RW_HD
    r2 = _iota((2 * n, LANES), 0)
    c2 = _iota((2 * n, LANES), 1)
    rowhead = (r2 < n) == (c2 < RW_HD)
    strict = jnp.logical_and(rowhead, c2 < r2)
    incl = jnp.logical_and(rowhead, c2 <= r2)
    masks = (m_top, rowhead, strict, incl)
    tri = _tri_incl(n)
    first_row = _iota((n, 1), 0) == 0

    def shift_mix(z, prev_row, mu):
        prev = jnp.where(first_row, prev_row, pltpu.roll(z, 1, 0))
        return z + (prev - z) * mu

    def chunk_body(c, _):
        r0 = pl.multiple_of(c * n, n)
        rows = pl.ds(r0, n)
        zr = zr_ref[rows, :]
        zk = zk_ref[rows, :]
        zv = zv_ref[rows, :]
        zl = zl_ref[rows, :]
        r = shift_mix(zr, carry[0:1, 0:D], pr_ref[_P_MUR:_P_MUR + 1, :])
        k = shift_mix(zk, carry[0:1, D:2 * D], pr_ref[_P_MUK:_P_MUK + 1, :])
        v = shift_mix(zv, carry[0:1, 2 * D:3 * D], pr_ref[_P_MUV:_P_MUV + 1, :])
        lo = shift_mix(zl, carry[0:1, 3 * D:3 * D + LANES], mul_ref[...])
        carry[0:1, 0:D] = zr[n - 1:n, :]
        carry[0:1, D:2 * D] = zk[n - 1:n, :]
        carry[0:1, 2 * D:3 * D] = zv[n - 1:n, :]
        carry[0:1, 3 * D:3 * D + LANES] = zl[n - 1:n, :]

        lo = jnp.where(_iota((n, LANES), 1) < 64, jnp.tanh(lo), lo)
        up = _mm3(lo, wl_ref[...])
        lw = -DECAY_SCALE * _sigmoid(pr_ref[_P_W0:_P_W0 + 1, :] + up[:, :D])
        a = _sigmoid(pr_ref[_P_A0:_P_A0 + 1, :] + up[:, D:])
        g = _mm_exact_lhs(tri, lw)
        cs[0] = r
        cs[1] = k * (1.0 + (a - 1.0) * pr_ref[_P_KA:_P_KA + 1, :])
        cs[2] = v
        cs[3] = lw
        cs[4] = g
        cs[5] = a
        cs[6] = k * pr_ref[_P_KK:_P_KK + 1, :]

        def pair_body(p, _):
            ln = pl.ds(pl.multiple_of(p * LANES, LANES), LANES)
            y, st_new = _rwkv_pair(
                cs[0, :, ln], cs[1, :, ln], cs[2, :, ln], cs[3, :, ln], cs[4, :, ln],
                cs[5, :, ln], cs[6, :, ln],
                pr_ref[_P_RK:_P_RK + 1, ln], pr_ref[_P_LNW:_P_LNW + 1, ln],
                pr_ref[_P_LNB:_P_LNB + 1, ln], st_ref[p], masks)
            st_ref[p] = st_new
            o_ref[rows, ln] = (y * _silu(gr_ref[rows, ln])).astype(o_ref.dtype)
            return 0

        lax.fori_loop(0, RW_PAIRS, pair_body, 0)
        return 0

    lax.fori_loop(0, TS // n, chunk_body, 0)


def _rwkv(zr, zk, zv, zl, gr, pr, mul, wl):
    bsz, s, _ = zr.shape
    seq = lambda w: pl.BlockSpec((None, TS, w), lambda b, i: (b, i, 0))
    return pl.pallas_call(
        _rwkv_body,
        grid=(bsz, s // TS),
        in_specs=[seq(D), seq(D), seq(D), seq(LANES), seq(D),
                  _const_spec(pr.shape), _const_spec(mul.shape), _const_spec(wl.shape)],
        out_specs=seq(D),
        out_shape=jax.ShapeDtypeStruct((bsz, s, D), BF16),
        scratch_shapes=[pltpu.VMEM((8, 3 * D + LANES), F32),
                        pltpu.VMEM((RW_PAIRS, LANES, LANES), F32),
                        pltpu.VMEM((7, CHUNK, D), F32)],
        compiler_params=_params(("parallel", "arbitrary")),
        name="rwkv",
    )(zr, zk, zv, zl, gr, pr, mul, wl)


def _log_sigmoid(x):
    return jnp.minimum(x, 0.0) - jnp.log(1.0 + jnp.exp(-jnp.abs(x)))


def _mlstm_body(xm_ref, gm_ref, cw_ref, cb_ref, cq_ref, ck_ref, cv_ref, wif_ref, wift_ref,
                bif_ref, bift_ref, nrm_ref, skp_ref, o_ref, prev8, ct_ref, n_ref, m_ref):
    n = CHUNK

    @pl.when(pl.program_id(1) == 0)
    def _():
        prev8[...] = jnp.zeros_like(prev8)
        ct_ref[...] = jnp.zeros_like(ct_ref)
        n_ref[...] = jnp.zeros_like(n_ref)
        m_ref[...] = jnp.zeros_like(m_ref)

    tri = _tri_incl(n)
    causal = _iota((n, n), 1) <= _iota((n, n), 0)

    def chunk_body(c, _):
        rows = pl.ds(pl.multiple_of(c * n, n), n)
        xm = xm_ref[rows, :]
        xc = _silu(_conv4(xm, prev8[...], cw_ref, cb_ref))
        prev8[...] = xm[n - 8:, :]
        q = _block4_linear(xc, cq_ref)
        km = _block4_linear(xc, ck_ref)
        vm = _block4_linear(xm, cv_ref)

        gc = bif_ref[...]
        gt = bift_ref[...]
        for j, t in enumerate((q, km, vm)):
            gc = gc + _mm3(t, wif_ref[j])
            gt = gt + _mm3(wift_ref[j], t, ((1,), (1,)))
        bc_c = _mm_exact_lhs(tri, _log_sigmoid(gc))
        bc_t = _mm_exact_rhs(_log_sigmoid(gt), tri, ((1,), (1,)))

        for h in range(ML_HEADS):
            ln = slice(h * ML_HD, (h + 1) * ML_HD)
            qh = q[:, ln]
            kh = km[:, ln] * (ML_HD ** -0.5)
            vh = vm[:, ln]
            li_c = gc[:, h:h + 1]
            li_t = gt[h:h + 1, :]
            b_c = bc_c[:, ML_HEADS + h:ML_HEADS + h + 1]
            b_t = bc_t[ML_HEADS + h:ML_HEADS + h + 1, :]
            m_prev = m_ref[h:h + 1, 0:1]
            ct = ct_ref[h]
            nv = n_ref[0:1, ln]

            log_d = jnp.where(causal, b_c - b_t + li_t, -jnp.inf)
            log_inter = b_c + m_prev
            m_t = jnp.maximum(jnp.max(log_d, axis=-1, keepdims=True), log_inter)
            s = _mm_nt(qh, kh) * jnp.exp(log_d - m_t)
            w_inter = jnp.exp(log_inter - m_t)
            num = _mm(s, vh) + w_inter * _mm(qh, ct)
            den = (jnp.sum(s, axis=-1, keepdims=True)
                   + w_inter * jnp.sum(qh * nv, axis=-1, keepdims=True))
            hh = num / (jnp.maximum(jnp.abs(den), jnp.exp(-m_t)) + ML_EPS)

            b_last = b_c[n - 1:n, :]
            log_g = b_last - b_c + li_c
            m_new = jnp.maximum(b_last + m_prev, jnp.max(log_g, axis=0, keepdims=True))
            gcol = jnp.exp(log_g - m_new)
            decay = jnp.exp(b_last + m_prev - m_new)
            ct_ref[h] = decay * ct + _mm_tn(kh, gcol * vh)
            n_ref[0:1, ln] = decay * nv + jnp.sum(gcol * kh, axis=0, keepdims=True)
            m_ref[h:h + 1, :] = jnp.broadcast_to(m_new, (1, LANES))

            mu = jnp.mean(hh, axis=-1, keepdims=True)
            dlt = hh - mu
            var = jnp.mean(dlt * dlt, axis=-1, keepdims=True)
            hn = dlt * lax.rsqrt(var + ML_LN_EPS) * nrm_ref[:, ln]
            ym = hn + skp_ref[:, ln] * xc[:, ln]
            o_ref[rows, ln] = (ym * _silu(gm_ref[rows, ln])).astype(o_ref.dtype)
        return 0

    lax.fori_loop(0, TS // n, chunk_body, 0)


def _mlstm(xm, gm, cw, cb, cq, ck, cv, wif, wift, bif, bift, nrm, skp):
    bsz, s, _ = xm.shape
    seq = pl.BlockSpec((None, TS, D), lambda b, i: (b, i, 0))
    consts = (cw, cb, cq, ck, cv, wif, wift, bif, bift, nrm, skp)
    return pl.pallas_call(
        _mlstm_body,
        grid=(bsz, s // TS),
        in_specs=[seq, seq] + [_const_spec(a.shape) for a in consts],
        out_specs=seq,
        out_shape=jax.ShapeDtypeStruct((bsz, s, D), BF16),
        scratch_shapes=[pltpu.VMEM((8, D), F32),
                        pltpu.VMEM((ML_HEADS, ML_HD, ML_HD), F32),
                        pltpu.VMEM((8, D), F32),
                        pltpu.VMEM((8, LANES), F32)],
        compiler_params=_params(("parallel", "arbitrary")),
        name="mlstm",
    )(xm, gm, *consts)


def _mix_out(h_ref, ya_ref, yb_ref, wo_ref, p_ref, pn_ref, pu_ref, pg_ref):
    h = h_ref[...] + _dg(ya_ref[...], wo_ref[:D, :], ((1,), (0,)))
    h = h + _dg(yb_ref[...], wo_ref[D:, :], ((1,), (0,)))
    hn = _rms(h, pn_ref[...]).astype(BF16)
    gate = _sigmoid(_dg(hn, pg_ref[...], ((1,), (0,))))
    return h + _dg(p_ref[...].astype(BF16), pu_ref[...], ((1,), (0,))) * gate


def _mid_body(h_ref, ya_ref, yb_ref, wo_ref, p_ref, pn_ref, pu_ref, pg_ref, mn_ref, wi_ref,
              ho_ref, xl_ref, g_ref):
    h = _mix_out(h_ref, ya_ref, yb_ref, wo_ref, p_ref, pn_ref, pu_ref, pg_ref)
    ho_ref[...] = h
    xn = _rms(h, mn_ref[...]).astype(BF16)
    xl_ref[...] = _dg(xn, wi_ref[:, :LRU_W], ((1,), (0,)))
    g_ref[...] = _dg(xn, wi_ref[:, LRU_W:], ((1,), (0,)))


def _final_body(h_ref, ya_ref, yb_ref, wo_ref, p_ref, pn_ref, pu_ref, pg_ref, fn_ref, o_ref):
    h = _mix_out(h_ref, ya_ref, yb_ref, wo_ref, p_ref, pn_ref, pu_ref, pg_ref)
    o_ref[...] = _rms(h, fn_ref[...])


def _row_spec(w, half=None):
    if half is None:
        return pl.BlockSpec((TM, w), lambda i: (i, 0))
    return pl.BlockSpec((TM, w), lambda i: (i, half))


def _mid(h, ya, yb, wo, p, pn, pu, pg, mn, wi):
    t = h.shape[0]
    return pl.pallas_call(
        _mid_body,
        grid=(t // TM,),
        in_specs=[_row_spec(D), _row_spec(D), _row_spec(D), _const_spec(wo.shape),
                  _row_spec(PLE), _const_spec(pn.shape), _const_spec(pu.shape),
                  _const_spec(pg.shape), _const_spec(mn.shape), _const_spec(wi.shape)],
        out_specs=[_row_spec(D), _row_spec(LRU_W), _row_spec(LRU_W)],
        out_shape=[jax.ShapeDtypeStruct((t, D), F32), jax.ShapeDtypeStruct((t, LRU_W), F32),
                   jax.ShapeDtypeStruct((t, LRU_W), F32)],
        compiler_params=_params(("parallel",)),
        name="mid",
    )(h, ya, yb, wo, p, pn, pu, pg, mn, wi)


def _final(h, y, wo, p, pn, pu, pg, fn):
    t = h.shape[0]
    return pl.pallas_call(
        _final_body,
        grid=(t // TM,),
        in_specs=[_row_spec(D), _row_spec(D, 0), _row_spec(D, 1), _const_spec(wo.shape),
                  _row_spec(PLE), _const_spec(pn.shape), _const_spec(pu.shape),
                  _const_spec(pg.shape), _const_spec(fn.shape)],
        out_specs=_row_spec(D),
        out_shape=jax.ShapeDtypeStruct((t, D), F32),
        compiler_params=_params(("parallel",)),
        name="final",
    )(h, y, y, wo, p, pn, pu, pg, fn)


def _rglru_body(xl_ref, g_ref, cw_ref, cb_ref, wr_ref, br_ref, wi_ref, bi_ref, lam_ref, o_ref,
                prev8, hc):
    n = CHUNK
    w = LRU_STRIP

    @pl.when(pl.program_id(2) == 0)
    def _():
        prev8[...] = jnp.zeros_like(prev8)
        hc[...] = jnp.zeros_like(hc)

    row = _iota((n, w), 0)

    def chunk_body(c, _):
        rows = pl.ds(pl.multiple_of(c * n, n), n)
        xl = xl_ref[rows, :]
        xc = _conv4(xl, prev8[...], cw_ref, cb_ref)
        prev8[...] = xl[n - 8:, :]
        rg = []
        ig = []
        for j in range(w // LRU_BLK):
            xb = xc[:, j * LRU_BLK:(j + 1) * LRU_BLK]
            rg.append(_mm(xb, wr_ref[j]))
            ig.append(_mm(xb, wi_ref[j]))
        rg = _sigmoid(jnp.concatenate(rg, axis=1) + br_ref[...])
        ig = _sigmoid(jnp.concatenate(ig, axis=1) + bi_ref[...])
        log_a = -LRU_C * rg * _softplus(-lam_ref[...])
        a = jnp.exp(log_a)
        mult = jnp.sqrt(jnp.maximum(1.0 - jnp.exp(2.0 * log_a), 0.0))
        u = xc * ig * mult
        d = 1
        while d < n:
            keep = row >= d
            u = u + a * jnp.where(keep, pltpu.roll(u, d, 0), 0.0)
            a = a * jnp.where(keep, pltpu.roll(a, d, 0), 1.0)
            d *= 2
        h = u + a * hc[0:1, :]
        hc[0:1, :] = h[n - 1:n, :]
        o_ref[rows, :] = (h * _silu(g_ref[rows, :])).astype(o_ref.dtype)
        return 0

    lax.fori_loop(0, TS // n, chunk_body, 0)


def _rglru(xl, g, cw, cb, wr, br, wi, bi, sp):
    bsz, s, _ = xl.shape
    ns = LRU_W // LRU_STRIP
    seq = pl.BlockSpec((None, TS, LRU_STRIP), lambda b, j, i: (b, i, j))
    vec = lambda r: pl.BlockSpec((r, LRU_STRIP), lambda b, j, i: (0, j))
    blk = pl.BlockSpec((LRU_STRIP // LRU_BLK, LRU_BLK, LRU_BLK), lambda b, j, i: (j, 0, 0))
    return pl.pallas_call(
        _rglru_body,
        grid=(bsz, ns, s // TS),
        in_specs=[seq, seq, vec(CONV_W), vec(1), blk, vec(1), blk, vec(1), vec(1)],
        out_specs=seq,
        out_shape=jax.ShapeDtypeStruct((bsz, s, LRU_W), BF16),
        scratch_shapes=[pltpu.VMEM((8, LRU_STRIP), F32), pltpu.VMEM((8, LRU_STRIP), F32)],
        compiler_params=_params(("parallel", "parallel", "arbitrary")),
        name="rglru",
    )(xl, g, cw, cb, wr, br, wi, bi, sp)


def _block4_coeffs(w):
    rows = []
    o = jnp.arange(QKV_BLOCK)
    for d in range(-3, 4):
        i = o - d
        ok = (i >= 0) & (i < QKV_BLOCK)
        coef = jnp.where(ok[None, :], w[:, jnp.clip(i, 0, QKV_BLOCK - 1), o], 0.0)
        rows.append(coef.reshape(-1))
    return jnp.stack(rows)


def kernel(x, p, mix_norm, pe_norm, final_norm, pe_up, pe_gate, ab_w_in, rwkv_mu, rwkv_mu_lora, rwkv_w0, rwkv_w_up, rwkv_a0, rwkv_a_up, rwkv_k_k, rwkv_k_a, rwkv_r_k, rwkv_ln_w, rwkv_ln_b, mlstm_conv_w, mlstm_conv_b, mlstm_wq, mlstm_wk, mlstm_wv, mlstm_w_if, mlstm_b_if, mlstm_norm, mlstm_skip, ab_w_out, c_w_in, c_conv_w, c_conv_b, c_wr, c_br, c_wi, c_bi, c_lambda, c_w_out):
    bsz, s, _ = x.shape
    t = bsz * s
    row = lambda a: a.reshape(1, -1)
    seq = lambda a: a.reshape(bsz, s, a.shape[-1])

    w = ab_w_in[0]
    w_in0 = jnp.concatenate([w[:, :3 * D], w[:, 3 * D + 128:], w[:, 3 * D:3 * D + 128]],
                            axis=1).astype(BF16)
    zr, zk, zv, xm, gr, gm, zl = _proj0(x.reshape(t, D), row(mix_norm[0]), w_in0)

    pr = jnp.concatenate([rwkv_mu[0], row(rwkv_w0[0]), row(rwkv_a0[0]), row(rwkv_k_k[0]),
                          row(rwkv_k_a[0]), row(rwkv_r_k[0]), row(rwkv_ln_w[0]),
                          row(rwkv_ln_b[0]), jnp.zeros((6, D), F32)], axis=0)
    mul = rwkv_mu_lora[0].reshape(1, 128)
    zeros = jnp.zeros((64, D), F32)
    wl = jnp.concatenate([jnp.concatenate([rwkv_w_up[0], zeros], axis=1),
                          jnp.concatenate([zeros, rwkv_a_up[0]], axis=1)], axis=0)
    y_rwkv = _rwkv(seq(zr), seq(zk), seq(zv), seq(zl), seq(gr), pr, mul, wl)

    wif = mlstm_w_if[0].reshape(3, D, 2 * ML_HEADS)
    y_m = _mlstm(seq(xm), seq(gm), mlstm_conv_w[0], row(mlstm_conv_b[0]),
                 _block4_coeffs(mlstm_wq[0]), _block4_coeffs(mlstm_wk[0]),
                 _block4_coeffs(mlstm_wv[0]), wif, jnp.swapaxes(wif, 1, 2),
                 row(mlstm_b_if[0]), mlstm_b_if[0].reshape(-1, 1),
                 row(mlstm_norm[0]), row(mlstm_skip[0]))

    h1, xl, g1 = _mid(x.reshape(t, D), y_rwkv.reshape(t, D), y_m.reshape(t, D),
                      ab_w_out[0].astype(BF16), p[0].reshape(t, PLE), row(pe_norm[0]),
                      pe_up[0].astype(BF16), pe_gate[0].astype(BF16), row(mix_norm[1]),
                      c_w_in[0].astype(BF16))

    y_lru = _rglru(seq(xl), seq(g1), c_conv_w[0], row(c_conv_b[0]), c_wr[0], row(c_br[0]),
                   c_wi[0], row(c_bi[0]), row(c_lambda[0]))
    out = _final(h1, y_lru.reshape(t, LRU_W), c_w_out[0].astype(BF16), p[1].reshape(t, PLE),
                 row(pe_norm[1]), pe_up[1].astype(BF16), pe_gate[1].astype(BF16),
                 row(final_norm))
    return out.reshape(bsz, s, D)
```

```python
import functools

import jax
import jax.numpy as jnp
import numpy as np
from jax import lax
from jax.experimental import pallas as pl
from jax.experimental.pallas import tpu as pltpu

F32 = jnp.float32
BF16 = jnp.bfloat16

D = 1024
LANES = 128
CHUNK = 64
NORM_EPS = 1e-6
PLE = 256
RW_HEADS = 16
RW_HD = 64
RW_PAIRS = RW_HEADS // 2
DECAY_SCALE = 0.6065306597126334
RW_GN_EPS = 64e-5
ML_HEADS = 4
ML_HD = 256
ML_EPS = 1e-6
ML_LN_EPS = 1e-5
QKV_BLOCK = 4
CONV_W = 4
LRU_W = 2 * D
LRU_BLK = 128
LRU_C = 8.0
AB_IN = 4 * D + 128 + 2 * D

TM = 256
TS = 512
LRU_STRIP = 512
VMEM_LIMIT = 56 * 1024 * 1024


def _dg(a, b, dims):
    return lax.dot_general(a, b, (dims, ((), ())), preferred_element_type=F32)


def _mm(a, b):
    return _dg(a.astype(BF16), b.astype(BF16), ((1,), (0,)))


def _mm_nt(a, b):
    return _dg(a.astype(BF16), b.astype(BF16), ((1,), (1,)))


def _mm_tn(a, b):
    return _dg(a.astype(BF16), b.astype(BF16), ((0,), (0,)))


def _split2(x):
    hi = x.astype(BF16)
    lo = (x - hi.astype(F32)).astype(BF16)
    return hi, lo


def _split3(x):
    hi = x.astype(BF16)
    r = x - hi.astype(F32)
    mid = r.astype(BF16)
    lo = (r - mid.astype(F32)).astype(BF16)
    return hi, mid, lo


def _mm3(a, b, dims=((1,), (0,))):
    ah, al = _split2(a)
    bh, bl = _split2(b)
    return _dg(ah, bh, dims) + (_dg(ah, bl, dims) + _dg(al, bh, dims))


def _mm_exact_lhs(a_bf16, b, dims=((1,), (0,))):
    h, m, l = _split3(b)
    return _dg(a_bf16, h, dims) + (_dg(a_bf16, m, dims) + _dg(a_bf16, l, dims))


def _mm_exact_rhs(a, b_bf16, dims=((1,), (0,))):
    h, m, l = _split3(a)
    return _dg(h, b_bf16, dims) + (_dg(m, b_bf16, dims) + _dg(l, b_bf16, dims))


def _rms(x, g):
    return x * lax.rsqrt(jnp.mean(x * x, axis=-1, keepdims=True) + NORM_EPS) * g


def _sigmoid(x):
    return 1.0 / (1.0 + jnp.exp(-x))


def _silu(x):
    return x * _sigmoid(x)


def _softplus(x):
    return jnp.maximum(x, 0.0) + jnp.log(1.0 + jnp.exp(-jnp.abs(x)))


def _iota(shape, axis):
    return lax.broadcasted_iota(jnp.int32, shape, axis)


def _tri_incl(n):
    return jnp.where(_iota((n, n), 1) <= _iota((n, n), 0), 1.0, 0.0).astype(BF16)


def _shift_rows(x, d, prev8):
    xs = pltpu.roll(x, d, 0)
    ps = pltpu.roll(prev8, d, 0)
    head = jnp.where(_iota((8, x.shape[1]), 0) < d, ps, xs[:8])
    return jnp.concatenate([head, xs[8:]], axis=0)


def _conv4(x, prev8, w_ref, b_ref):
    acc = x * w_ref[3:4, :] + b_ref[...]
    for d in (1, 2, 3):
        acc = acc + _shift_rows(x, d, prev8) * w_ref[3 - d:4 - d, :]
    return acc


def _block4_linear(x, c_ref):
    n = x.shape[1]
    acc = x * c_ref[3:4, :]
    for d in (-3, -2, -1, 1, 2, 3):
        acc = acc + pltpu.roll(x, d % n, 1) * c_ref[d + 3:d + 4, :]
    return acc


def _const_spec(shape):
    nd = len(shape)
    return pl.BlockSpec(shape, lambda *_: (0,) * nd, pipeline_mode=pl.Buffered(1))


def _params(sem):
    return pltpu.CompilerParams(dimension_semantics=sem, vmem_limit_bytes=VMEM_LIMIT)


def _proj0_body(x_ref, g_ref, w_ref, zr, zk, zv, xm, gr, gm, zl):
    xn = _rms(x_ref[...], g_ref[...]).astype(BF16)
    for j, o in enumerate((zr, zk, zv, xm, gr, gm)):
        o[...] = _dg(xn, w_ref[:, j * D:(j + 1) * D], ((1,), (0,)))
    zl[...] = _dg(xn, w_ref[:, 6 * D:6 * D + 128], ((1,), (0,)))


def _proj0(x2, g, w):
    t = x2.shape[0]
    wide = jax.ShapeDtypeStruct((t, D), F32)
    row = pl.BlockSpec((TM, D), lambda i: (i, 0))
    return pl.pallas_call(
        _proj0_body,
        grid=(t // TM,),
        in_specs=[row, _const_spec((1, D)), _const_spec((D, AB_IN))],
        out_specs=[row] * 6 + [pl.BlockSpec((TM, 128), lambda i: (i, 0))],
        out_shape=[wide] * 6 + [jax.ShapeDtypeStruct((t, 128), F32)],
        compiler_params=_params(("parallel",)),
        name="proj0",
    )(x2, g, w)


_P_MUR, _P_MUK, _P_MUV, _P_W0, _P_A0, _P_KK, _P_KA, _P_RK, _P_LNW, _P_LNB = range(10)


def _rwkv_pairs(ops, sts, masks):
    m_top, rowhead, strict, incl = masks
    n = CHUNK
    np_ = len(ops)
    rng = range(np_)
    bf = lambda t: t.astype(BF16)
    nn, nt, tn = ((1,), (0,)), ((1,), (1,)), ((0,), (0,))

    def stack(x):
        return jnp.concatenate([jnp.where(m_top, x, 0.0), jnp.where(m_top, 0.0, x)], axis=0)

    def unstack_col(c):
        return jnp.where(m_top, c[:n], c[n:])

    pre = []
    for r, ke, v, lw, g, a, kkraw, rk, lnw, lnb in ops:
        norm = jnp.sqrt(jnp.sum(jnp.square(stack(kkraw)), axis=-1, keepdims=True))
        kk = kkraw / unstack_col(jnp.maximum(norm, 1e-12))
        bonus = unstack_col(jnp.sum(stack(r * ke * rk), axis=-1, keepdims=True))
        g_last = g[n - 1:n, :]
        b = kk * a
        e_neg = jnp.exp(-g)
        e_tail = jnp.exp(g_last - g)
        ag_f = stack(-kk * jnp.exp(g - lw))
        pre.append(dict(
            bonus=bonus, v=v, lnw=lnw, lnb=lnb,
            rg_s=bf(stack(r * jnp.exp(g))), ag_f=ag_f, ag_s=bf(ag_f),
            bi=bf(b * e_neg), ki=bf(ke * e_neg), bd_s=bf(stack(b * e_tail)),
            kd=bf(ke * e_tail), vb=bf(v),
            e_last_col=jnp.transpose(jnp.broadcast_to(jnp.exp(g_last), (LANES, LANES)))))

    sc = [_dg(jnp.concatenate([q["ag_s"], q["rg_s"]], axis=0),
              jnp.concatenate([q["bi"], q["bi"], q["ki"], q["ki"]], axis=0), nt) for q in pre]
    n_ab = [bf(jnp.where(strict, t[:2 * n, :2 * n], 0.0)) for t in sc]
    n_ak = [bf(jnp.where(strict, t[:2 * n, 2 * n:], 0.0)) for t in sc]
    a_rb = [bf(jnp.where(incl, t[2 * n:, :2 * n], 0.0)) for t in sc]
    a_rk = [bf(jnp.where(incl, t[2 * n:, 2 * n:], 0.0)) for t in sc]

    v2 = [jnp.concatenate([q["vb"], q["vb"]], axis=0) for q in pre]
    akv = [jnp.where(rowhead, _dg(n_ak[i], v2[i], nn), 0.0) for i in rng]
    arkv = [jnp.where(rowhead, _dg(a_rk[i], v2[i], nn), 0.0) for i in rng]
    kv = [jnp.where(rowhead, _dg(q["kd"], q["vb"], tn), 0.0) for q in pre]

    x = [jnp.concatenate([pre[i]["ag_f"], akv[i]], axis=1) for i in rng]
    nk = n_ab
    for step in range(6):
        x = [x[i] + _dg(nk[i], bf(x[i]), nn) for i in rng]
        if step < 5:
            nk = [bf(_dg(t, t, nn)) for t in nk]

    stb = [bf(t) for t in sts]
    u = [_dg(bf(x[i][:, :LANES]), stb[i], nn) + x[i][:, LANES:] for i in rng]
    ub = [bf(t) for t in u]
    st_new = [pre[i]["e_last_col"] * sts[i] + _dg(pre[i]["bd_s"], ub[i], tn) + kv[i]
              for i in rng]
    y_s = [_dg(pre[i]["rg_s"], stb[i], nn) + _dg(a_rb[i], ub[i], nn) + arkv[i] for i in rng]

    ys = []
    for i in rng:
        q = pre[i]
        mu = jnp.sum(y_s[i], axis=-1, keepdims=True) * (1.0 / RW_HD)
        dlt = jnp.where(rowhead, y_s[i] - mu, 0.0)
        var = jnp.sum(dlt * dlt, axis=-1, keepdims=True) * (1.0 / RW_HD)
        yn = dlt * lax.rsqrt(var + RW_GN_EPS)
        ys.append((yn[:n] + yn[n:]) * q["lnw"] + q["lnb"] + q["bonus"] * q["v"])
    return ys, st_new


def _rwkv_body(zr_ref, zk_ref, zv_ref, zl_ref, gr_ref, pr_ref, mul_ref, wl_ref, o_ref,
               carry, st_ref):
    n = CHUNK

    @pl.when(pl.program_id(1) == 0)
    def _():
        carry[...] = jnp.zeros_like(carry)
        st_ref[...] = jnp.zeros_like(st_ref)

    lane = _iota((1, LANES), 1)
    m_top = lane < RW_HD
    r2 = _iota((2 * n, LANES), 0)
    c2 = _iota((2 * n, LANES), 1)
    rowhead = (r2 < n) == (c2 < RW_HD)
    strict = jnp.logical_and(rowhead, c2 < r2)
    incl = jnp.logical_and(rowhead, c2 <= r2)
    masks = (m_top, rowhead, strict, incl)
    tri = _tri_incl(n)
    first_row = _iota((n, 1), 0) == 0

    def shift_mix(z, prev_row, mu):
        prev = jnp.where(first_row, prev_row, pltpu.roll(z, 1, 0))
        return z + (prev - z) * mu

    def chunk_body(c, _):
        r0 = pl.multiple_of(c * n, n)
        rows = pl.ds(r0, n)
        zr = zr_ref[rows, :]
        zk = zk_ref[rows, :]
        zv = zv_ref[rows, :]
        zl = zl_ref[rows, :]
        r = shift_mix(zr, carry[0:1, 0:D], pr_ref[_P_MUR:_P_MUR + 1, :])
        k = shift_mix(zk, carry[0:1, D:2 * D], pr_ref[_P_MUK:_P_MUK + 1, :])
        v = shift_mix(zv, carry[0:1, 2 * D:3 * D], pr_ref[_P_MUV:_P_MUV + 1, :])
        lo = shift_mix(zl, carry[0:1, 3 * D:3 * D + LANES], mul_ref[...])
        carry[0:1, 0:D] = zr[n - 1:n, :]
        carry[0:1, D:2 * D] = zk[n - 1:n, :]
        carry[0:1, 2 * D:3 * D] = zv[n - 1:n, :]
        carry[0:1, 3 * D:3 * D + LANES] = zl[n - 1:n, :]

        lo = jnp.where(_iota((n, LANES), 1) < 64, jnp.tanh(lo), lo)
        up = _mm3(lo, wl_ref[...])
        lw = -DECAY_SCALE * _sigmoid(pr_ref[_P_W0:_P_W0 + 1, :] + up[:, :D])
        a = _sigmoid(pr_ref[_P_A0:_P_A0 + 1, :] + up[:, D:])
        g = _mm_exact_lhs(tri, lw)
        ke = k * (1.0 + (a - 1.0) * pr_ref[_P_KA:_P_KA + 1, :])
        kkraw = k * pr_ref[_P_KK:_P_KK + 1, :]
        lanes = [slice(p * LANES, (p + 1) * LANES) for p in range(RW_PAIRS)]
        ops = [(r[:, ln], ke[:, ln], v[:, ln], lw[:, ln], g[:, ln], a[:, ln], kkraw[:, ln],
                pr_ref[_P_RK:_P_RK + 1, ln], pr_ref[_P_LNW:_P_LNW + 1, ln],
                pr_ref[_P_LNB:_P_LNB + 1, ln]) for ln in lanes]
        ys, st_new = _rwkv_pairs(ops, [st_ref[p] for p in range(RW_PAIRS)], masks)
        for p, ln in enumerate(lanes):
            st_ref[p] = st_new[p]
            o_ref[rows, ln] = (ys[p] * _silu(gr_ref[rows, ln])).astype(o_ref.dtype)
        return 0

    lax.fori_loop(0, TS // n, chunk_body, 0)


def _rwkv(zr, zk, zv, zl, gr, pr, mul, wl):
    bsz, s, _ = zr.shape
    seq = lambda w: pl.BlockSpec((None, TS, w), lambda b, i: (b, i, 0))
    return pl.pallas_call(
        _rwkv_body,
        grid=(bsz, s // TS),
        in_specs=[seq(D), seq(D), seq(D), seq(LANES), seq(D),
                  _const_spec(pr.shape), _const_spec(mul.shape), _const_spec(wl.shape)],
        out_specs=seq(D),
        out_shape=jax.ShapeDtypeStruct((bsz, s, D), BF16),
        scratch_shapes=[pltpu.VMEM((8, 3 * D + LANES), F32),
                        pltpu.VMEM((RW_PAIRS, LANES, LANES), F32)],
        compiler_params=_params(("parallel", "arbitrary")),
        name="rwkv",
    )(zr, zk, zv, zl, gr, pr, mul, wl)


def _log_sigmoid(x):
    return jnp.minimum(x, 0.0) - jnp.log(1.0 + jnp.exp(-jnp.abs(x)))


def _mlstm_body(xm_ref, gm_ref, cw_ref, cb_ref, cq_ref, ck_ref, cv_ref, wif_ref, wift_ref,
                bif_ref, bift_ref, nrm_ref, skp_ref, o_ref, prev8, ct_ref, n_ref, m_ref):
    n = CHUNK

    @pl.when(pl.program_id(1) == 0)
    def _():
        prev8[...] = jnp.zeros_like(prev8)
        ct_ref[...] = jnp.zeros_like(ct_ref)
        n_ref[...] = jnp.zeros_like(n_ref)
        m_ref[...] = jnp.zeros_like(m_ref)

    tri = _tri_incl(n)
    causal = _iota((n, n), 1) <= _iota((n, n), 0)

    def chunk_body(c, _):
        rows = pl.ds(pl.multiple_of(c * n, n), n)
        xm = xm_ref[rows, :]
        xc = _silu(_conv4(xm, prev8[...], cw_ref, cb_ref))
        prev8[...] = xm[n - 8:, :]
        q = _block4_linear(xc, cq_ref)
        km = _block4_linear(xc, ck_ref)
        vm = _block4_linear(xm, cv_ref)

        gc = bif_ref[...]
        gt = bift_ref[...]
        for j, t in enumerate((q, km, vm)):
            gc = gc + _mm3(t, wif_ref[j])
            gt = gt + _mm3(wift_ref[j], t, ((1,), (1,)))
        bc_c = _mm_exact_lhs(tri, _log_sigmoid(gc))
        bc_t = _mm_exact_rhs(_log_sigmoid(gt), tri, ((1,), (1,)))

        for h in range(ML_HEADS):
            ln = slice(h * ML_HD, (h + 1) * ML_HD)
            qh = q[:, ln]
            kh = km[:, ln] * (ML_HD ** -0.5)
            vh = vm[:, ln]
            li_c = gc[:, h:h + 1]
            li_t = gt[h:h + 1, :]
            b_c = bc_c[:, ML_HEADS + h:ML_HEADS + h + 1]
            b_t = bc_t[ML_HEADS + h:ML_HEADS + h + 1, :]
            m_prev = m_ref[h:h + 1, 0:1]
            ct = ct_ref[h]
            nv = n_ref[0:1, ln]

            log_d = jnp.where(causal, b_c - b_t + li_t, -jnp.inf)
            log_inter = b_c + m_prev
            m_t = jnp.maximum(jnp.max(log_d, axis=-1, keepdims=True), log_inter)
            s = _mm_nt(qh, kh) * jnp.exp(log_d - m_t)
            w_inter = jnp.exp(log_inter - m_t)
            num = _mm(s, vh) + w_inter * _mm(qh, ct)
            den = (jnp.sum(s, axis=-1, keepdims=True)
                   + w_inter * jnp.sum(qh * nv, axis=-1, keepdims=True))
            hh = num / (jnp.maximum(jnp.abs(den), jnp.exp(-m_t)) + ML_EPS)

            b_last = b_c[n - 1:n, :]
            log_g = b_last - b_c + li_c
            m_new = jnp.maximum(b_last + m_prev, jnp.max(log_g, axis=0, keepdims=True))
            gcol = jnp.exp(log_g - m_new)
            decay = jnp.exp(b_last + m_prev - m_new)
            ct_ref[h] = decay * ct + _mm_tn(kh, gcol * vh)
            n_ref[0:1, ln] = decay * nv + jnp.sum(gcol * kh, axis=0, keepdims=True)
            m_ref[h:h + 1, :] = jnp.broadcast_to(m_new, (1, LANES))

            mu = jnp.mean(hh, axis=-1, keepdims=True)
            dlt = hh - mu
            var = jnp.mean(dlt * dlt, axis=-1, keepdims=True)
            hn = dlt * lax.rsqrt(var + ML_LN_EPS) * nrm_ref[:, ln]
            ym = hn + skp_ref[:, ln] * xc[:, ln]
            o_ref[rows, ln] = (ym * _silu(gm_ref[rows, ln])).astype(o_ref.dtype)
        return 0

    lax.fori_loop(0, TS // n, chunk_body, 0)


def _mlstm(xm, gm, cw, cb, cq, ck, cv, wif, wift, bif, bift, nrm, skp):
    bsz, s, _ = xm.shape
    seq = pl.BlockSpec((None, TS, D), lambda b, i: (b, i, 0))
    consts = (cw, cb, cq, ck, cv, wif, wift, bif, bift, nrm, skp)
    return pl.pallas_call(
        _mlstm_body,
        grid=(bsz, s // TS),
        in_specs=[seq, seq] + [_const_spec(a.shape) for a in consts],
        out_specs=seq,
        out_shape=jax.ShapeDtypeStruct((bsz, s, D), BF16),
        scratch_shapes=[pltpu.VMEM((8, D), F32),
                        pltpu.VMEM((ML_HEADS, ML_HD, ML_HD), F32),
                        pltpu.VMEM((8, D), F32),
                        pltpu.VMEM((8, LANES), F32)],
        compiler_params=_params(("parallel", "arbitrary")),
        name="mlstm",
    )(xm, gm, *consts)


def _mix_out(h_ref, ya_ref, yb_ref, wo_ref, p_ref, pn_ref, pu_ref, pg_ref):
    h = h_ref[...] + _dg(ya_ref[...], wo_ref[:D, :], ((1,), (0,)))
    h = h + _dg(yb_ref[...], wo_ref[D:, :], ((1,), (0,)))
    hn = _rms(h, pn_ref[...]).astype(BF16)
    gate = _sigmoid(_dg(hn, pg_ref[...], ((1,), (0,))))
    return h + _dg(p_ref[...].astype(BF16), pu_ref[...], ((1,), (0,))) * gate


def _mid_body(h_ref, ya_ref, yb_ref, wo_ref, p_ref, pn_ref, pu_ref, pg_ref, mn_ref, wi_ref,
              ho_ref, xl_ref, g_ref):
    h = _mix_out(h_ref, ya_ref, yb_ref, wo_ref, p_ref, pn_ref, pu_ref, pg_ref)
    ho_ref[...] = h
    xn = _rms(h, mn_ref[...]).astype(BF16)
    xl_ref[...] = _dg(xn, wi_ref[:, :LRU_W], ((1,), (0,)))
    g_ref[...] = _dg(xn, wi_ref[:, LRU_W:], ((1,), (0,)))


def _final_body(h_ref, ya_ref, yb_ref, wo_ref, p_ref, pn_ref, pu_ref, pg_ref, fn_ref, o_ref):
    h = _mix_out(h_ref, ya_ref, yb_ref, wo_ref, p_ref, pn_ref, pu_ref, pg_ref)
    o_ref[...] = _rms(h, fn_ref[...])


def _row_spec(w, half=None):
    if half is None:
        return pl.BlockSpec((TM, w), lambda i: (i, 0))
    return pl.BlockSpec((TM, w), lambda i: (i, half))


def _mid(h, ya, yb, wo, p, pn, pu, pg, mn, wi):
    t = h.shape[0]
    return pl.pallas_call(
        _mid_body,
        grid=(t // TM,),
        in_specs=[_row_spec(D), _row_spec(D), _row_spec(D), _const_spec(wo.shape),
                  _row_spec(PLE), _const_spec(pn.shape), _const_spec(pu.shape),
                  _const_spec(pg.shape), _const_spec(mn.shape), _const_spec(wi.shape)],
        out_specs=[_row_spec(D), _row_spec(LRU_W), _row_spec(LRU_W)],
        out_shape=[jax.ShapeDtypeStruct((t, D), F32), jax.ShapeDtypeStruct((t, LRU_W), F32),
                   jax.ShapeDtypeStruct((t, LRU_W), F32)],
        compiler_params=_params(("parallel",)),
        name="mid",
    )(h, ya, yb, wo, p, pn, pu, pg, mn, wi)


def _final(h, y, wo, p, pn, pu, pg, fn):
    t = h.shape[0]
    return pl.pallas_call(
        _final_body,
        grid=(t // TM,),
        in_specs=[_row_spec(D), _row_spec(D, 0), _row_spec(D, 1), _const_spec(wo.shape),
                  _row_spec(PLE), _const_spec(pn.shape), _const_spec(pu.shape),
                  _const_spec(pg.shape), _const_spec(fn.shape)],
        out_specs=_row_spec(D),
        out_shape=jax.ShapeDtypeStruct((t, D), F32),
        compiler_params=_params(("parallel",)),
        name="final",
    )(h, y, y, wo, p, pn, pu, pg, fn)


def _rglru_body(xl_ref, g_ref, cw_ref, cb_ref, wr_ref, br_ref, wi_ref, bi_ref, lam_ref, o_ref,
                prev8, hc):
    n = CHUNK
    w = LRU_STRIP

    @pl.when(pl.program_id(2) == 0)
    def _():
        prev8[...] = jnp.zeros_like(prev8)
        hc[...] = jnp.zeros_like(hc)

    row = _iota((n, w), 0)

    def chunk_body(c, _):
        rows = pl.ds(pl.multiple_of(c * n, n), n)
        xl = xl_ref[rows, :]
        xc = _conv4(xl, prev8[...], cw_ref, cb_ref)
        prev8[...] = xl[n - 8:, :]
        rg = []
        ig = []
        for j in range(w // LRU_BLK):
            xb = xc[:, j * LRU_BLK:(j + 1) * LRU_BLK]
            rg.append(_mm(xb, wr_ref[j]))
            ig.append(_mm(xb, wi_ref[j]))
        rg = _sigmoid(jnp.concatenate(rg, axis=1) + br_ref[...])
        ig = _sigmoid(jnp.concatenate(ig, axis=1) + bi_ref[...])
        log_a = -LRU_C * rg * _softplus(-lam_ref[...])
        a = jnp.exp(log_a)
        mult = jnp.sqrt(jnp.maximum(1.0 - jnp.exp(2.0 * log_a), 0.0))
        u = xc * ig * mult
        d = 1
        while d < n:
            keep = row >= d
            u = u + a * jnp.where(keep, pltpu.roll(u, d, 0), 0.0)
            a = a * jnp.where(keep, pltpu.roll(a, d, 0), 1.0)
            d *= 2
        h = u + a * hc[0:1, :]
        hc[0:1, :] = h[n - 1:n, :]
        o_ref[rows, :] = (h * _silu(g_ref[rows, :])).astype(o_ref.dtype)
        return 0

    lax.fori_loop(0, TS // n, chunk_body, 0)


def _rglru(xl, g, cw, cb, wr, br, wi, bi, sp):
    bsz, s, _ = xl.shape
    ns = LRU_W // LRU_STRIP
    seq = pl.BlockSpec((None, TS, LRU_STRIP), lambda b, j, i: (b, i, j))
    vec = lambda r: pl.BlockSpec((r, LRU_STRIP), lambda b, j, i: (0, j))
    blk = pl.BlockSpec((LRU_STRIP // LRU_BLK, LRU_BLK, LRU_BLK), lambda b, j, i: (j, 0, 0))
    return pl.pallas_call(
        _rglru_body,
        grid=(bsz, ns, s // TS),
        in_specs=[seq, seq, vec(CONV_W), vec(1), blk, vec(1), blk, vec(1), vec(1)],
        out_specs=seq,
        out_shape=jax.ShapeDtypeStruct((bsz, s, LRU_W), BF16),
        scratch_shapes=[pltpu.VMEM((8, LRU_STRIP), F32), pltpu.VMEM((8, LRU_STRIP), F32)],
        compiler_params=_params(("parallel", "parallel", "arbitrary")),
        name="rglru",
    )(xl, g, cw, cb, wr, br, wi, bi, sp)


def _block4_coeffs(w):
    rows = []
    o = jnp.arange(QKV_BLOCK)
    for d in range(-3, 4):
        i = o - d
        ok = (i >= 0) & (i < QKV_BLOCK)
        coef = jnp.where(ok[None, :], w[:, jnp.clip(i, 0, QKV_BLOCK - 1), o], 0.0)
        rows.append(coef.reshape(-1))
    return jnp.stack(rows)


def kernel(x, p, mix_norm, pe_norm, final_norm, pe_up, pe_gate, ab_w_in, rwkv_mu, rwkv_mu_lora, rwkv_w0, rwkv_w_up, rwkv_a0, rwkv_a_up, rwkv_k_k, rwkv_k_a, rwkv_r_k, rwkv_ln_w, rwkv_ln_b, mlstm_conv_w, mlstm_conv_b, mlstm_wq, mlstm_wk, mlstm_wv, mlstm_w_if, mlstm_b_if, mlstm_norm, mlstm_skip, ab_w_out, c_w_in, c_conv_w, c_conv_b, c_wr, c_br, c_wi, c_bi, c_lambda, c_w_out):
    bsz, s, _ = x.shape
    t = bsz * s
    row = lambda a: a.reshape(1, -1)
    seq = lambda a: a.reshape(bsz, s, a.shape[-1])

    w = ab_w_in[0]
    w_in0 = jnp.concatenate([w[:, :3 * D], w[:, 3 * D + 128:], w[:, 3 * D:3 * D + 128]],
                            axis=1).astype(BF16)
    zr, zk, zv, xm, gr, gm, zl = _proj0(x.reshape(t, D), row(mix_norm[0]), w_in0)

    pr = jnp.concatenate([rwkv_mu[0], row(rwkv_w0[0]), row(rwkv_a0[0]), row(rwkv_k_k[0]),
                          row(rwkv_k_a[0]), row(rwkv_r_k[0]), row(rwkv_ln_w[0]),
                          row(rwkv_ln_b[0]), jnp.zeros((6, D), F32)], axis=0)
    mul = rwkv_mu_lora[0].reshape(1, 128)
    zeros = jnp.zeros((64, D), F32)
    wl = jnp.concatenate([jnp.concatenate([rwkv_w_up[0], zeros], axis=1),
                          jnp.concatenate([zeros, rwkv_a_up[0]], axis=1)], axis=0)
    y_rwkv = _rwkv(seq(zr), seq(zk), seq(zv), seq(zl), seq(gr), pr, mul, wl)

    wif = mlstm_w_if[0].reshape(3, D, 2 * ML_HEADS)
    y_m = _mlstm(seq(xm), seq(gm), mlstm_conv_w[0], row(mlstm_conv_b[0]),
                 _block4_coeffs(mlstm_wq[0]), _block4_coeffs(mlstm_wk[0]),
                 _block4_coeffs(mlstm_wv[0]), wif, jnp.swapaxes(wif, 1, 2),
                 row(mlstm_b_if[0]), mlstm_b_if[0].reshape(-1, 1),
                 row(mlstm_norm[0]), row(mlstm_skip[0]))

    h1, xl, g1 = _mid(x.reshape(t, D), y_rwkv.reshape(t, D), y_m.reshape(t, D),
                      ab_w_out[0].astype(BF16), p[0].reshape(t, PLE), row(pe_norm[0]),
                      pe_up[0].astype(BF16), pe_gate[0].astype(BF16), row(mix_norm[1]),
                      c_w_in[0].astype(BF16))

    y_lru = _rglru(seq(xl), seq(g1), c_conv_w[0], row(c_conv_b[0]), c_wr[0], row(c_br[0]),
                   c_wi[0], row(c_bi[0]), row(c_lambda[0]))
    out = _final(h1, y_lru.reshape(t, LRU_W), c_w_out[0].astype(BF16), p[1].reshape(t, PLE),
                 row(pe_norm[1]), pe_up[1].astype(BF16), pe_gate[1].astype(BF16),
                 row(final_norm))
    return out.reshape(bsz, s, D)
```

```python
import jax
import jax.numpy as jnp
from jax import lax
from jax.experimental import pallas as pl
from jax.experimental.pallas import tpu as pltpu

F32 = jnp.float32
BF16 = jnp.bfloat16

D = 1024
LANES = 128
CHUNK = 64
NORM_EPS = 1e-6
PLE = 256
RW_HEADS = 16
RW_HD = 64
RW_PAIRS = RW_HEADS // 2
DECAY_SCALE = 0.6065306597126334
RW_GN_EPS = 64e-5
ML_HEADS = 4
ML_HD = 256
ML_EPS = 1e-6
ML_LN_EPS = 1e-5
QKV_BLOCK = 4
CONV_W = 4
LRU_W = 2 * D
LRU_BLK = 128
LRU_C = 8.0
AB_IN = 4 * D + 128 + 2 * D

TM = 256
TS = 512
LRU_STRIP = 512
RW_UNROLL = 2
RW_GROUPS = 1
RW_SKEW = 3
ML_UNROLL = 4
ML_SKEW = 2
LRU_UNROLL = 2
VMEM_LIMIT = 56 * 1024 * 1024

assert CHUNK == RW_HD and ML_SKEW >= 2 and RW_GROUPS * RW_SKEW >= 1


def _dg(a, b, dims):
    return lax.dot_general(a, b, (dims, ((), ())), preferred_element_type=F32)


def _split2(x):
    hi = x.astype(BF16)
    lo = (x - hi.astype(F32)).astype(BF16)
    return hi, lo


def _split3(x):
    hi = x.astype(BF16)
    r = x - hi.astype(F32)
    mid = r.astype(BF16)
    lo = (r - mid.astype(F32)).astype(BF16)
    return hi, mid, lo


def _mm3(a, b, dims=((1,), (0,))):
    ah, al = _split2(a)
    bh, bl = _split2(b)
    return _dg(ah, bh, dims) + (_dg(ah, bl, dims) + _dg(al, bh, dims))


def _mm_exact_lhs(a_bf16, b, dims=((1,), (0,))):
    h, m, l = _split3(b)
    return _dg(a_bf16, h, dims) + (_dg(a_bf16, m, dims) + _dg(a_bf16, l, dims))


def _mm_exact_rhs(a, b_bf16, dims=((1,), (0,))):
    h, m, l = _split3(a)
    return _dg(h, b_bf16, dims) + (_dg(m, b_bf16, dims) + _dg(l, b_bf16, dims))


def _rms(x, g):
    return x * lax.rsqrt(jnp.mean(x * x, axis=-1, keepdims=True) + NORM_EPS) * g


def _sigmoid(x):
    return 1.0 / (1.0 + jnp.exp(-x))


def _silu(x):
    return x * _sigmoid(x)


def _softplus(x):
    return jnp.maximum(x, 0.0) + jnp.log(1.0 + jnp.exp(-jnp.abs(x)))


def _iota(shape, axis):
    return lax.broadcasted_iota(jnp.int32, shape, axis)


def _tri_incl(n):
    return jnp.where(_iota((n, n), 1) <= _iota((n, n), 0), 1.0, 0.0).astype(BF16)


def _shift_rows(x, d, prev8):
    xs = pltpu.roll(x, d, 0)
    ps = pltpu.roll(prev8, d, 0)
    head = jnp.where(_iota((8, x.shape[1]), 0) < d, ps, xs[:8])
    return jnp.concatenate([head, xs[8:]], axis=0)


def _conv4(x, prev8, w_ref, b_ref):
    acc = x * w_ref[3:4, :] + b_ref[...]
    for d in (1, 2, 3):
        acc = acc + _shift_rows(x, d, prev8) * w_ref[3 - d:4 - d, :]
    return acc


def _run_skewed(tasks, skew):
    pending = list(tasks)
    active = []
    tick = 0
    while pending or active:
        if pending and tick % skew == 0:
            active.append(pending.pop(0))
        for t in list(active):
            try:
                next(t)
            except StopIteration:
                active.remove(t)
        tick += 1


def _const_spec(shape):
    nd = len(shape)
    return pl.BlockSpec(shape, lambda *_: (0,) * nd, pipeline_mode=pl.Buffered(1))


def _params(sem):
    return pltpu.CompilerParams(dimension_semantics=sem, vmem_limit_bytes=VMEM_LIMIT)


def _proj0_body(x_ref, g_ref, w_ref, zr, zk, zv, xm, gr, gm, zl):
    xn = _rms(x_ref[...], g_ref[...]).astype(BF16)
    for j, o in enumerate((zr, zk, zv, xm, gr, gm)):
        o[...] = _dg(xn, w_ref[:, j * D:(j + 1) * D], ((1,), (0,)))
    zl[...] = _dg(xn, w_ref[:, 6 * D:6 * D + 128], ((1,), (0,)))


def _proj0(x2, g, w):
    t = x2.shape[0]
    wide = jax.ShapeDtypeStruct((t, D), F32)
    row = pl.BlockSpec((TM, D), lambda i: (i, 0))
    return pl.pallas_call(
        _proj0_body,
        grid=(t // TM,),
        in_specs=[row, _const_spec((1, D)), _const_spec((D, AB_IN))],
        out_specs=[row] * 6 + [pl.BlockSpec((TM, 128), lambda i: (i, 0))],
        out_shape=[wide] * 6 + [jax.ShapeDtypeStruct((t, 128), F32)],
        compiler_params=_params(("parallel",)),
        name="proj0",
    )(x2, g, w)


_P_MUR, _P_MUK, _P_MUV, _P_W0, _P_A0, _P_KK, _P_KA, _P_RK, _P_LNW, _P_LNB = range(10)


def _rwkv_task(get_ops, pairs, state, masks, emit):
    m_top, rowhead, strict, incl, eye = masks
    n = CHUNK
    rng = range(len(pairs))
    bf = lambda t: t.astype(BF16)
    nn, nt, tn = ((1,), (0,)), ((1,), (1,)), ((0,), (0,))

    def stack(x):
        return jnp.concatenate([jnp.where(m_top, x, 0.0), jnp.where(m_top, 0.0, x)], axis=0)

    def unstack_col(c):
        return jnp.where(m_top, c[:n], c[n:])

    pre = []
    for r, ke, v, lw, g, a, kkraw, rk, lnw, lnb in get_ops():
        norm = jnp.sqrt(jnp.sum(jnp.square(stack(kkraw)), axis=-1, keepdims=True))
        kk = kkraw / unstack_col(jnp.maximum(norm, 1e-12))
        bonus = unstack_col(jnp.sum(stack(r * ke * rk), axis=-1, keepdims=True))
        g_last = g[n - 1:n, :]
        b = kk * a
        e_neg = jnp.exp(-g)
        e_tail = jnp.exp(g_last - g)
        ag_f = stack(-kk * jnp.exp(g - lw))
        pre.append(dict(
            bonus=bonus, v=v, lnw=lnw, lnb=lnb,
            rg_s=bf(stack(r * jnp.exp(g))), ag_s=bf(ag_f),
            bi=bf(b * e_neg), ki=bf(ke * e_neg), bd_s=bf(stack(b * e_tail)),
            kd=bf(ke * e_tail), vb=bf(v),
            e_last_col=jnp.transpose(jnp.broadcast_to(jnp.exp(g_last), (LANES, LANES)))))
    yield

    sc = [_dg(jnp.concatenate([q["ag_s"], q["rg_s"]], axis=0),
              jnp.concatenate([q["bi"], q["bi"], q["ki"], q["ki"]], axis=0), nt) for q in pre]
    n_ab_f = [jnp.where(strict, t[:2 * n, :2 * n], 0.0) for t in sc]
    n_ab = [bf(t) for t in n_ab_f]
    n_ak = [bf(jnp.where(strict, t[:2 * n, 2 * n:], 0.0)) for t in sc]
    a_rb = [bf(jnp.where(incl, t[2 * n:, :2 * n], 0.0)) for t in sc]
    a_rk = [bf(jnp.where(incl, t[2 * n:, 2 * n:], 0.0)) for t in sc]
    yield

    v2 = [jnp.concatenate([q["vb"], q["vb"]], axis=0) for q in pre]
    akv = [jnp.where(rowhead, _dg(n_ak[i], v2[i], nn), 0.0) for i in rng]
    arkv = [jnp.where(rowhead, _dg(a_rk[i], v2[i], nn), 0.0) for i in rng]
    yield

    pk = n_ab
    t = [eye + n_ab_f[i] for i in rng]
    p_next = [bf(_dg(pk[i], pk[i], nn)) for i in rng]
    yield
    for step in range(1, 6):
        pk = p_next
        if step < 5:
            m = [_dg(pk[i], jnp.concatenate([bf(t[i]), pk[i]], axis=1), nn) for i in rng]
            p_next = [bf(m[i][:, LANES:]) for i in rng]
            t = [t[i] + m[i][:, :LANES] for i in rng]
        else:
            t = [t[i] + _dg(pk[i], bf(t[i]), nn) for i in rng]
        yield

    x = [_dg(bf(t[i]), jnp.concatenate([pre[i]["ag_s"], bf(akv[i])], axis=1), nn) for i in rng]
    yield

    sts = [state[p] for p in pairs]
    stb = [bf(s_) for s_ in sts]
    u = [_dg(bf(x[i][:, :LANES]), stb[i], nn) + x[i][:, LANES:] for i in rng]
    ub = [bf(u_) for u_ in u]
    yield

    for i, p in enumerate(pairs):
        upd = _dg(jnp.concatenate([pre[i]["bd_s"], pre[i]["kd"]], axis=0),
                  jnp.concatenate([ub[i], pre[i]["vb"]], axis=0), tn)
        state[p] = pre[i]["e_last_col"] * sts[i] + jnp.where(rowhead, upd, 0.0)
    y_s = [_dg(jnp.concatenate([pre[i]["rg_s"], a_rb[i]], axis=1),
               jnp.concatenate([stb[i], ub[i]], axis=0), nn) + arkv[i] for i in rng]
    yield

    for i, p in enumerate(pairs):
        q = pre[i]
        mu = jnp.sum(y_s[i], axis=-1, keepdims=True) * (1.0 / RW_HD)
        dlt = jnp.where(rowhead, y_s[i] - mu, 0.0)
        var = jnp.sum(dlt * dlt, axis=-1, keepdims=True) * (1.0 / RW_HD)
        yn = dlt * lax.rsqrt(var + RW_GN_EPS)
        emit(p, (yn[:n] + yn[n:]) * q["lnw"] + q["lnb"] + q["bonus"] * q["v"])


def _rwkv_body(zr_ref, zk_ref, zv_ref, zl_ref, gr_ref, pr_ref, mul_ref, wl_ref, o_ref,
               carry, st_ref):
    n = CHUNK

    @pl.when(pl.program_id(1) == 0)
    def _():
        carry[...] = jnp.zeros_like(carry)
        st_ref[...] = jnp.zeros_like(st_ref)

    lane = _iota((1, LANES), 1)
    m_top = lane < RW_HD
    r2 = _iota((2 * n, LANES), 0)
    c2 = _iota((2 * n, LANES), 1)
    rowhead = (r2 < n) == (c2 < RW_HD)
    strict = jnp.logical_and(rowhead, c2 < r2)
    incl = jnp.logical_and(rowhead, c2 <= r2)
    eye = jnp.where(r2 == c2, 1.0, 0.0)
    masks = (m_top, rowhead, strict, incl, eye)
    tri = _tri_incl(n)
    first_row = _iota((n, 1), 0) == 0

    def shift_mix(z, prev_row, mu):
        prev = jnp.where(first_row, prev_row, pltpu.roll(z, 1, 0))
        return z + (prev - z) * mu

    def body(it, _):
        state = {p: st_ref[p] for p in range(RW_PAIRS)}
        last = dict(r=carry[0:1, 0:D], k=carry[0:1, D:2 * D], v=carry[0:1, 2 * D:3 * D],
                    l=carry[0:1, 3 * D:3 * D + LANES])
        chunk_vals = {}

        def chunk_level(j):
            if j in chunk_vals:
                return chunk_vals[j]
            rows = pl.ds(pl.multiple_of((it * RW_UNROLL + j) * n, n), n)
            zr = zr_ref[rows, :]
            zk = zk_ref[rows, :]
            zv = zv_ref[rows, :]
            zl = zl_ref[rows, :]
            r = shift_mix(zr, last["r"], pr_ref[_P_MUR:_P_MUR + 1, :])
            k = shift_mix(zk, last["k"], pr_ref[_P_MUK:_P_MUK + 1, :])
            v = shift_mix(zv, last["v"], pr_ref[_P_MUV:_P_MUV + 1, :])
            lo = shift_mix(zl, last["l"], mul_ref[...])
            last.update(r=zr[n - 1:n, :], k=zk[n - 1:n, :], v=zv[n - 1:n, :], l=zl[n - 1:n, :])
            lo = jnp.where(_iota((n, LANES), 1) < 64, jnp.tanh(lo), lo)
            up = _mm3(lo, wl_ref[...])
            lw = -DECAY_SCALE * _sigmoid(pr_ref[_P_W0:_P_W0 + 1, :] + up[:, :D])
            a = _sigmoid(pr_ref[_P_A0:_P_A0 + 1, :] + up[:, D:])
            g = _mm_exact_lhs(tri, lw)
            ke = k * (1.0 + (a - 1.0) * pr_ref[_P_KA:_P_KA + 1, :])
            kkraw = k * pr_ref[_P_KK:_P_KK + 1, :]
            chunk_vals[j] = (rows, (r, ke, v, lw, g, a, kkraw))
            return chunk_vals[j]

        def make_task(j, pairs):
            def get_ops():
                _, arrs = chunk_level(j)
                out = []
                for p in pairs:
                    ln = slice(p * LANES, (p + 1) * LANES)
                    out.append(tuple(t[:, ln] for t in arrs) + (
                        pr_ref[_P_RK:_P_RK + 1, ln], pr_ref[_P_LNW:_P_LNW + 1, ln],
                        pr_ref[_P_LNB:_P_LNB + 1, ln]))
                return out

            def emit(p, y):
                rows, _ = chunk_level(j)
                ln = slice(p * LANES, (p + 1) * LANES)
                o_ref[rows, ln] = (y * _silu(gr_ref[rows, ln])).astype(o_ref.dtype)

            return _rwkv_task(get_ops, pairs, state, masks, emit)

        per_group = RW_PAIRS // RW_GROUPS
        groups = [list(range(gi * per_group, (gi + 1) * per_group)) for gi in range(RW_GROUPS)]
        _run_skewed([make_task(j, grp) for j in range(RW_UNROLL) for grp in groups], RW_SKEW)

        for p in range(RW_PAIRS):
            st_ref[p] = state[p]
        carry[0:1, 0:D] = last["r"]
        carry[0:1, D:2 * D] = last["k"]
        carry[0:1, 2 * D:3 * D] = last["v"]
        carry[0:1, 3 * D:3 * D + LANES] = last["l"]
        return 0

    lax.fori_loop(0, TS // (n * RW_UNROLL), body, 0)


def _rwkv(zr, zk, zv, zl, gr, pr, mul, wl):
    bsz, s, _ = zr.shape
    seq = lambda w: pl.BlockSpec((None, TS, w), lambda b, i: (b, i, 0))
    return pl.pallas_call(
        _rwkv_body,
        grid=(bsz, s // TS),
        in_specs=[seq(D), seq(D), seq(D), seq(LANES), seq(D),
                  _const_spec(pr.shape), _const_spec(mul.shape), _const_spec(wl.shape)],
        out_specs=seq(D),
        out_shape=jax.ShapeDtypeStruct((bsz, s, D), BF16),
        scratch_shapes=[pltpu.VMEM((8, 3 * D + LANES), F32),
                        pltpu.VMEM((RW_PAIRS, LANES, LANES), F32)],
        compiler_params=_params(("parallel", "arbitrary")),
        name="rwkv",
    )(zr, zk, zv, zl, gr, pr, mul, wl)


def _log_sigmoid(x):
    return jnp.minimum(x, 0.0) - jnp.log(1.0 + jnp.exp(-jnp.abs(x)))


def _mlstm_task(rows, refs, state, consts):
    (xm_ref, gm_ref, cw_ref, cb_ref, wq_ref, wk_ref, wv_ref, wif_ref, wift_ref, bif_ref,
     bift_ref, nrm_ref, skp_ref, o_ref) = refs
    tri, causal = consts
    n = CHUNK
    nn, nt, tn = ((1,), (0,)), ((1,), (1,)), ((0,), (0,))
    hs = range(ML_HEADS)
    lns = [slice(h * ML_HD, (h + 1) * ML_HD) for h in hs]
    k_scale = ML_HD ** -0.5

    xm = xm_ref[rows, :]
    xc = _silu(_conv4(xm, state["prev8"], cw_ref, cb_ref))
    state["prev8"] = xm[n - 8:, :]
    yield

    xcb = xc.astype(BF16)
    xmb = xm.astype(BF16)
    qb = [_dg(xcb[:, ln], wq_ref[h], nn).astype(BF16) for h, ln in enumerate(lns)]
    kf = [_dg(xcb[:, ln], wk_ref[h], nn) for h, ln in enumerate(lns)]
    vf = [_dg(xmb[:, ln], wv_ref[h], nn) for h, ln in enumerate(lns)]
    kb = [t.astype(BF16) for t in kf]
    vb = [t.astype(BF16) for t in vf]
    yield

    gc = bif_ref[...]
    gt = bift_ref[...]
    for j, ts in enumerate((qb, kb, vb)):
        for h in hs:
            gc = gc + _dg(ts[h], wif_ref[j, lns[h], :], nn)
            gt = gt + _dg(wift_ref[j, :, lns[h]], ts[h], nt)
    bc_c = _mm_exact_lhs(tri, _log_sigmoid(gc))
    bc_t = _mm_exact_rhs(_log_sigmoid(gt), tri, nt)
    li_c = [gc[:, h:h + 1] for h in hs]
    li_t = [gt[h:h + 1, :] for h in hs]
    b_c = [bc_c[:, ML_HEADS + h:ML_HEADS + h + 1] for h in hs]
    b_t = [bc_t[ML_HEADS + h:ML_HEADS + h + 1, :] for h in hs]
    qk = [_dg(qb[h], kb[h], nt) for h in hs]
    log_d = [jnp.where(causal, b_c[h] - b_t[h] + li_t[h], -jnp.inf) for h in hs]
    b_last = [b_c[h][n - 1:n, :] for h in hs]
    log_g = [b_last[h] - b_c[h] + li_c[h] for h in hs]
    yield

    m_prev = [state["m", h] for h in hs]
    ct = [state["ct", h] for h in hs]
    nv = [state["nv", h] for h in hs]
    qc = [_dg(qb[h], ct[h].astype(BF16), nn) for h in hs]
    log_inter = [b_c[h] + m_prev[h] for h in hs]
    m_t = [jnp.maximum(jnp.max(log_d[h], axis=-1, keepdims=True), log_inter[h]) for h in hs]
    s = [qk[h] * k_scale * jnp.exp(log_d[h] - m_t[h]) for h in hs]
    w_inter = [jnp.exp(log_inter[h] - m_t[h]) for h in hs]
    m_new = [jnp.maximum(b_last[h] + m_prev[h], jnp.max(log_g[h], axis=0, keepdims=True))
             for h in hs]
    gcol = [jnp.exp(log_g[h] - m_new[h]) for h in hs]
    decay = [jnp.exp(b_last[h] + m_prev[h] - m_new[h]) for h in hs]
    yield

    sv = [_dg(s[h].astype(BF16), vb[h], nn) for h in hs]
    kgv = [_dg(kb[h], (gcol[h] * vf[h]).astype(BF16), tn) for h in hs]
    yield

    for h in hs:
        state["ct", h] = decay[h] * ct[h] + kgv[h] * k_scale
        state["nv", h] = decay[h] * nv[h] + k_scale * jnp.sum(gcol[h] * kf[h], axis=0,
                                                              keepdims=True)
        state["m", h] = m_new[h]
    yield

    for h in hs:
        ln = lns[h]
        num = sv[h] + w_inter[h] * qc[h]
        den = (jnp.sum(s[h], axis=-1, keepdims=True)
               + w_inter[h] * jnp.sum(qb[h].astype(F32) * nv[h], axis=-1, keepdims=True))
        hh = num / (jnp.maximum(jnp.abs(den), jnp.exp(-m_t[h])) + ML_EPS)
        mu = jnp.mean(hh, axis=-1, keepdims=True)
        dlt = hh - mu
        var = jnp.mean(dlt * dlt, axis=-1, keepdims=True)
        hn = dlt * lax.rsqrt(var + ML_LN_EPS) * nrm_ref[:, ln]
        ym = hn + skp_ref[:, ln] * xc[:, ln]
        o_ref[rows, ln] = (ym * _silu(gm_ref[rows, ln])).astype(o_ref.dtype)


def _mlstm_body(xm_ref, gm_ref, cw_ref, cb_ref, wq_ref, wk_ref, wv_ref, wif_ref, wift_ref,
                bif_ref, bift_ref, nrm_ref, skp_ref, o_ref, prev8, ct_ref, n_ref, m_ref):
    n = CHUNK

    @pl.when(pl.program_id(1) == 0)
    def _():
        prev8[...] = jnp.zeros_like(prev8)
        ct_ref[...] = jnp.zeros_like(ct_ref)
        n_ref[...] = jnp.zeros_like(n_ref)
        m_ref[...] = jnp.zeros_like(m_ref)

    consts = (_tri_incl(n), _iota((n, n), 1) <= _iota((n, n), 0))
    refs = (xm_ref, gm_ref, cw_ref, cb_ref, wq_ref, wk_ref, wv_ref, wif_ref, wift_ref, bif_ref,
            bift_ref, nrm_ref, skp_ref, o_ref)

    def body(it, _):
        state = {"prev8": prev8[...]}
        for h in range(ML_HEADS):
            state["ct", h] = ct_ref[h]
            state["nv", h] = n_ref[0:1, h * ML_HD:(h + 1) * ML_HD]
            state["m", h] = m_ref[h:h + 1, 0:1]
        tasks = [_mlstm_task(pl.ds(pl.multiple_of((it * ML_UNROLL + j) * n, n), n), refs, state,
                             consts) for j in range(ML_UNROLL)]
        _run_skewed(tasks, ML_SKEW)
        prev8[...] = state["prev8"]
        for h in range(ML_HEADS):
            ct_ref[h] = state["ct", h]
            n_ref[0:1, h * ML_HD:(h + 1) * ML_HD] = state["nv", h]
            m_ref[h:h + 1, :] = jnp.broadcast_to(state["m", h], (1, LANES))
        return 0

    lax.fori_loop(0, TS // (n * ML_UNROLL), body, 0)


def _mlstm(xm, gm, cw, cb, wq, wk, wv, wif, wift, bif, bift, nrm, skp):
    bsz, s, _ = xm.shape
    seq = pl.BlockSpec((None, TS, D), lambda b, i: (b, i, 0))
    consts = (cw, cb, wq, wk, wv, wif, wift, bif, bift, nrm, skp)
    return pl.pallas_call(
        _mlstm_body,
        grid=(bsz, s // TS),
        in_specs=[seq, seq] + [_const_spec(a.shape) for a in consts],
        out_specs=seq,
        out_shape=jax.ShapeDtypeStruct((bsz, s, D), BF16),
        scratch_shapes=[pltpu.VMEM((8, D), F32),
                        pltpu.VMEM((ML_HEADS, ML_HD, ML_HD), F32),
                        pltpu.VMEM((8, D), F32),
                        pltpu.VMEM((8, LANES), F32)],
        compiler_params=_params(("parallel", "arbitrary")),
        name="mlstm",
    )(xm, gm, *consts)


def _mix_out(h_ref, ya_ref, yb_ref, wo_ref, p_ref, pn_ref, pu_ref, pg_ref):
    h = h_ref[...] + _dg(ya_ref[...], wo_ref[:D, :], ((1,), (0,)))
    h = h + _dg(yb_ref[...], wo_ref[D:, :], ((1,), (0,)))
    hn = _rms(h, pn_ref[...]).astype(BF16)
    gate = _sigmoid(_dg(hn, pg_ref[...], ((1,), (0,))))
    return h + _dg(p_ref[...].astype(BF16), pu_ref[...], ((1,), (0,))) * gate


def _mid_body(h_ref, ya_ref, yb_ref, wo_ref, p_ref, pn_ref, pu_ref, pg_ref, mn_ref, wi_ref,
              ho_ref, xl_ref, g_ref):
    h = _mix_out(h_ref, ya_ref, yb_ref, wo_ref, p_ref, pn_ref, pu_ref, pg_ref)
    ho_ref[...] = h
    xn = _rms(h, mn_ref[...]).astype(BF16)
    xl_ref[...] = _dg(xn, wi_ref[:, :LRU_W], ((1,), (0,)))
    g_ref[...] = _dg(xn, wi_ref[:, LRU_W:], ((1,), (0,)))


def _final_body(h_ref, ya_ref, yb_ref, wo_ref, p_ref, pn_ref, pu_ref, pg_ref, fn_ref, o_ref):
    h = _mix_out(h_ref, ya_ref, yb_ref, wo_ref, p_ref, pn_ref, pu_ref, pg_ref)
    o_ref[...] = _rms(h, fn_ref[...])


def _row_spec(w, half=None):
    if half is None:
        return pl.BlockSpec((TM, w), lambda i: (i, 0))
    return pl.BlockSpec((TM, w), lambda i: (i, half))


def _mid(h, ya, yb, wo, p, pn, pu, pg, mn, wi):
    t = h.shape[0]
    return pl.pallas_call(
        _mid_body,
        grid=(t // TM,),
        in_specs=[_row_spec(D), _row_spec(D), _row_spec(D), _const_spec(wo.shape),
                  _row_spec(PLE), _const_spec(pn.shape), _const_spec(pu.shape),
                  _const_spec(pg.shape), _const_spec(mn.shape), _const_spec(wi.shape)],
        out_specs=[_row_spec(D), _row_spec(LRU_W), _row_spec(LRU_W)],
        out_shape=[jax.ShapeDtypeStruct((t, D), F32), jax.ShapeDtypeStruct((t, LRU_W), F32),
                   jax.ShapeDtypeStruct((t, LRU_W), F32)],
        compiler_params=_params(("parallel",)),
        name="mid",
    )(h, ya, yb, wo, p, pn, pu, pg, mn, wi)


def _final(h, y, wo, p, pn, pu, pg, fn):
    t = h.shape[0]
    return pl.pallas_call(
        _final_body,
        grid=(t // TM,),
        in_specs=[_row_spec(D), _row_spec(D, 0), _row_spec(D, 1), _const_spec(wo.shape),
                  _row_spec(PLE), _const_spec(pn.shape), _const_spec(pu.shape),
                  _const_spec(pg.shape), _const_spec(fn.shape)],
        out_specs=_row_spec(D),
        out_shape=jax.ShapeDtypeStruct((t, D), F32),
        compiler_params=_params(("parallel",)),
        name="final",
    )(h, y, y, wo, p, pn, pu, pg, fn)


def _rglru_body(xl_ref, g_ref, cw_ref, cb_ref, wr_ref, br_ref, wi_ref, bi_ref, lam_ref, o_ref,
                prev8, hc):
    n = CHUNK
    w = LRU_STRIP

    @pl.when(pl.program_id(2) == 0)
    def _():
        prev8[...] = jnp.zeros_like(prev8)
        hc[...] = jnp.zeros_like(hc)

    row8 = _iota((n // 8, 8, w), 1)

    def chunk_body(c, _):
        rows = pl.ds(pl.multiple_of(c * n, n), n)
        xl = xl_ref[rows, :]
        xc = _conv4(xl, prev8[...], cw_ref, cb_ref)
        prev8[...] = xl[n - 8:, :]
        rg = []
        ig = []
        for j in range(w // LRU_BLK):
            xb = xc[:, j * LRU_BLK:(j + 1) * LRU_BLK].astype(BF16)
            rg.append(_dg(xb, wr_ref[j], ((1,), (0,))))
            ig.append(_dg(xb, wi_ref[j], ((1,), (0,))))
        rg = _sigmoid(jnp.concatenate(rg, axis=1) + br_ref[...])
        ig = _sigmoid(jnp.concatenate(ig, axis=1) + bi_ref[...])
        log_a = -LRU_C * rg * _softplus(-lam_ref[...])
        a = jnp.exp(log_a)
        mult = jnp.sqrt(jnp.maximum(1.0 - jnp.exp(2.0 * log_a), 0.0))
        u = xc * ig * mult
        u = u.reshape(n // 8, 8, w)
        a = a.reshape(n // 8, 8, w)
        for d in (1, 2, 4):
            keep = row8 >= d
            u = u + a * jnp.where(keep, pltpu.roll(u, d, 1), 0.0)
            a = a * jnp.where(keep, pltpu.roll(a, d, 1), 1.0)
        carry = hc[0:1, :]
        hs = []
        for j in range(n // 8):
            hj = u[j] + a[j] * carry
            carry = hj[7:8, :]
            hs.append(hj)
        hc[0:1, :] = carry
        h = jnp.concatenate(hs, axis=0)
        o_ref[rows, :] = (h * _silu(g_ref[rows, :])).astype(o_ref.dtype)
        return 0

    lax.fori_loop(0, TS // n, chunk_body, 0, unroll=LRU_UNROLL)


def _rglru(xl, g, cw, cb, wr, br, wi, bi, lam):
    bsz, s, _ = xl.shape
    ns = LRU_W // LRU_STRIP
    seq = pl.BlockSpec((None, TS, LRU_STRIP), lambda b, j, i: (b, i, j))
    vec = lambda r: pl.BlockSpec((r, LRU_STRIP), lambda b, j, i: (0, j))
    blk = pl.BlockSpec((LRU_STRIP // LRU_BLK, LRU_BLK, LRU_BLK), lambda b, j, i: (j, 0, 0))
    return pl.pallas_call(
        _rglru_body,
        grid=(bsz, ns, s // TS),
        in_specs=[seq, seq, vec(CONV_W), vec(1), blk, vec(1), blk, vec(1), vec(1)],
        out_specs=seq,
        out_shape=jax.ShapeDtypeStruct((bsz, s, LRU_W), BF16),
        scratch_shapes=[pltpu.VMEM((8, LRU_STRIP), F32), pltpu.VMEM((8, LRU_STRIP), F32)],
        compiler_params=_params(("parallel", "parallel", "arbitrary")),
        name="rglru",
    )(xl, g, cw, cb, wr, br, wi, bi, lam)


def _block4_tiles(w):
    nblk = w.shape[0]
    per = ML_HD // QKV_BLOCK
    eye = jnp.eye(per, dtype=w.dtype)
    wt = w.reshape(nblk // per, per, QKV_BLOCK, QKV_BLOCK)
    dense = jnp.einsum("tgio,gh->tgiho", wt, eye)
    return dense.reshape(nblk // per, ML_HD, ML_HD)


def kernel(x, p, mix_norm, pe_norm, final_norm, pe_up, pe_gate, ab_w_in, rwkv_mu, rwkv_mu_lora, rwkv_w0, rwkv_w_up, rwkv_a0, rwkv_a_up, rwkv_k_k, rwkv_k_a, rwkv_r_k, rwkv_ln_w, rwkv_ln_b, mlstm_conv_w, mlstm_conv_b, mlstm_wq, mlstm_wk, mlstm_wv, mlstm_w_if, mlstm_b_if, mlstm_norm, mlstm_skip, ab_w_out, c_w_in, c_conv_w, c_conv_b, c_wr, c_br, c_wi, c_bi, c_lambda, c_w_out):
    bsz, s, _ = x.shape
    t = bsz * s
    row = lambda a: a.reshape(1, -1)
    seq = lambda a: a.reshape(bsz, s, a.shape[-1])

    w = ab_w_in[0]
    w_in0 = jnp.concatenate([w[:, :3 * D], w[:, 3 * D + 128:], w[:, 3 * D:3 * D + 128]],
                            axis=1).astype(BF16)
    zr, zk, zv, xm, gr, gm, zl = _proj0(x.reshape(t, D), row(mix_norm[0]), w_in0)

    pr = jnp.concatenate([rwkv_mu[0], row(rwkv_w0[0]), row(rwkv_a0[0]), row(rwkv_k_k[0]),
                          row(rwkv_k_a[0]), row(rwkv_r_k[0]), row(rwkv_ln_w[0]),
                          row(rwkv_ln_b[0]), jnp.zeros((6, D), F32)], axis=0)
    mul = rwkv_mu_lora[0].reshape(1, 128)
    zeros = jnp.zeros((64, D), F32)
    wl = jnp.concatenate([jnp.concatenate([rwkv_w_up[0], zeros], axis=1),
                          jnp.concatenate([zeros, rwkv_a_up[0]], axis=1)], axis=0)
    y_rwkv = _rwkv(seq(zr), seq(zk), seq(zv), seq(zl), seq(gr), pr, mul, wl)

    wif = mlstm_w_if[0].reshape(3, D, 2 * ML_HEADS)
    y_m = _mlstm(seq(xm), seq(gm), mlstm_conv_w[0], row(mlstm_conv_b[0]),
                 _block4_tiles(mlstm_wq[0]).astype(BF16), _block4_tiles(mlstm_wk[0]).astype(BF16),
                 _block4_tiles(mlstm_wv[0]).astype(BF16), wif.astype(BF16),
                 jnp.swapaxes(wif, 1, 2).astype(BF16),
                 row(mlstm_b_if[0]), mlstm_b_if[0].reshape(-1, 1),
                 row(mlstm_norm[0]), row(mlstm_skip[0]))

    h1, xl, g1 = _mid(x.reshape(t, D), y_rwkv.reshape(t, D), y_m.reshape(t, D),
                      ab_w_out[0].astype(BF16), p[0].reshape(t, PLE), row(pe_norm[0]),
                      pe_up[0].astype(BF16), pe_gate[0].astype(BF16), row(mix_norm[1]),
                      c_w_in[0].astype(BF16))

    y_lru = _rglru(seq(xl), seq(g1), c_conv_w[0], row(c_conv_b[0]), c_wr[0].astype(BF16),
                   row(c_br[0]), c_wi[0].astype(BF16), row(c_bi[0]), row(c_lambda[0]))
    out = _final(h1, y_lru.reshape(t, LRU_W), c_w_out[0].astype(BF16), p[1].reshape(t, PLE),
                 row(pe_norm[1]), pe_up[1].astype(BF16), pe_gate[1].astype(BF16),
                 row(final_norm))
    return out.reshape(bsz, s, D)
```

```python
import jax
import jax.numpy as jnp
from jax import lax
from jax.experimental import pallas as pl
from jax.experimental.pallas import tpu as pltpu

F32 = jnp.float32
BF16 = jnp.bfloat16

D = 1024
LANES = 128
CHUNK = 64
NORM_EPS = 1e-6
PLE = 256
RW_HEADS = 16
RW_HD = 64
RW_PAIRS = RW_HEADS // 2
DECAY_SCALE = 0.6065306597126334
RW_GN_EPS = 64e-5
ML_HEADS = 4
ML_HD = 256
ML_EPS = 1e-6
ML_LN_EPS = 1e-5
QKV_BLOCK = 4
CONV_W = 4
LRU_W = 2 * D
LRU_BLK = 128
LRU_C = 8.0
AB_IN = 4 * D + 128 + 2 * D

TM = 512
TS = 512
LRU_STRIP = 512
RW_UNROLL = 4
RW_GROUPS = 1
RW_SKEW = 2
ML_UNROLL = 4
ML_SKEW = 2
LRU_UNROLL = 4
VMEM_LIMIT = 56 * 1024 * 1024

assert CHUNK == RW_HD and ML_SKEW >= 2 and RW_GROUPS * RW_SKEW >= 1


def _dg(a, b, dims):
    return lax.dot_general(a, b, (dims, ((), ())), preferred_element_type=F32)


def _split2(x):
    hi = x.astype(BF16)
    lo = (x - hi.astype(F32)).astype(BF16)
    return hi, lo


def _split3(x):
    hi = x.astype(BF16)
    r = x - hi.astype(F32)
    mid = r.astype(BF16)
    lo = (r - mid.astype(F32)).astype(BF16)
    return hi, mid, lo


def _cumsum_rows(tri_bf16, x):
    hi, lo = _split2(x)
    return _dg(tri_bf16, hi, ((1,), (0,))) + _dg(tri_bf16, lo, ((1,), (0,)))


def _mm_exact_lhs(a_bf16, b, dims=((1,), (0,))):
    h, m, l = _split3(b)
    return _dg(a_bf16, h, dims) + (_dg(a_bf16, m, dims) + _dg(a_bf16, l, dims))


def _mm_exact_rhs(a, b_bf16, dims=((1,), (0,))):
    h, m, l = _split3(a)
    return _dg(h, b_bf16, dims) + (_dg(m, b_bf16, dims) + _dg(l, b_bf16, dims))


def _rms(x, g):
    return x * lax.rsqrt(jnp.mean(x * x, axis=-1, keepdims=True) + NORM_EPS) * g


def _sigmoid(x):
    return 1.0 / (1.0 + jnp.exp(-x))


def _silu(x):
    return x * _sigmoid(x)


def _softplus(x):
    return jnp.maximum(x, 0.0) + jnp.log(1.0 + jnp.exp(-jnp.abs(x)))


def _iota(shape, axis):
    return lax.broadcasted_iota(jnp.int32, shape, axis)


def _tri_incl(n):
    return jnp.where(_iota((n, n), 1) <= _iota((n, n), 0), 1.0, 0.0).astype(BF16)


def _shift_rows(x, d, prev8):
    xs = pltpu.roll(x, d, 0)
    ps = pltpu.roll(prev8, d, 0)
    head = jnp.where(_iota((8, x.shape[1]), 0) < d, ps, xs[:8])
    return jnp.concatenate([head, xs[8:]], axis=0)


def _conv4(x, prev8, w_ref, b_ref):
    acc = x * w_ref[3:4, :] + b_ref[...]
    for d in (1, 2, 3):
        acc = acc + _shift_rows(x, d, prev8) * w_ref[3 - d:4 - d, :]
    return acc


def _run_skewed(tasks, skew):
    pending = list(tasks)
    active = []
    tick = 0
    while pending or active:
        if pending and tick % skew == 0:
            active.append(pending.pop(0))
        for t in list(active):
            try:
                next(t)
            except StopIteration:
                active.remove(t)
        tick += 1


def _const_spec(shape):
    nd = len(shape)
    return pl.BlockSpec(shape, lambda *_: (0,) * nd, pipeline_mode=pl.Buffered(1))


def _params(sem):
    return pltpu.CompilerParams(dimension_semantics=sem, vmem_limit_bytes=VMEM_LIMIT)


def _proj0_body(x_ref, g_ref, w_ref, zr, zk, zv, xm, gr, gm, zl):
    xn = _rms(x_ref[...], g_ref[...]).astype(BF16)
    for j, o in enumerate((zr, zk, zv, xm, gr, gm)):
        o[...] = _dg(xn, w_ref[:, j * D:(j + 1) * D], ((1,), (0,)))
    zl[...] = _dg(xn, w_ref[:, 6 * D:6 * D + 128], ((1,), (0,)))


def _proj0(x2, g, w):
    t = x2.shape[0]
    wide = jax.ShapeDtypeStruct((t, D), F32)
    row = pl.BlockSpec((TM, D), lambda i: (i, 0))
    return pl.pallas_call(
        _proj0_body,
        grid=(t // TM,),
        in_specs=[row, _const_spec((1, D)), _const_spec((D, AB_IN))],
        out_specs=[row] * 6 + [pl.BlockSpec((TM, 128), lambda i: (i, 0))],
        out_shape=[wide] * 6 + [jax.ShapeDtypeStruct((t, 128), F32)],
        compiler_params=_params(("parallel",)),
        name="proj0",
    )(x2, g, w)


_P_MUR, _P_MUK, _P_MUV, _P_W0, _P_A0, _P_KK, _P_KA, _P_RK, _P_LNW, _P_LNB = range(10)


def _rwkv_task(get_ops, pairs, state, masks, emit):
    m_top, rowhead, strict, incl, eye = masks
    n = CHUNK
    rng = range(len(pairs))
    bf = lambda t: t.astype(BF16)
    nn, nt, tn = ((1,), (0,)), ((1,), (1,)), ((0,), (0,))

    def stack(x):
        z = jnp.zeros_like(x)
        return jnp.concatenate([jnp.where(m_top, x, z), jnp.where(m_top, z, x)], axis=0)

    def unstack_col(c):
        return jnp.where(m_top, c[:n], c[n:])

    def head_sum(x):
        return unstack_col(jnp.sum(stack(x), axis=-1, keepdims=True))

    pre = []
    for r, ke, v, lw, g, a, kkraw, rk, lnw, lnb in get_ops():
        kk = kkraw / jnp.maximum(jnp.sqrt(head_sum(jnp.square(kkraw))), 1e-12)
        bonus = head_sum(r * ke * rk)
        g_last = g[n - 1:n, :]
        b = kk * a
        e_neg = jnp.exp(-g)
        e_tail = jnp.exp(g_last - g)
        ag = bf(-kk * jnp.exp(g - lw))
        vb = bf(v)
        pre.append(dict(
            bonus=bonus, v=v, lnw=lnw, lnb=lnb, vb=vb, ag=ag, rg=bf(r * jnp.exp(g)),
            bk_s=jnp.concatenate([stack(bf(b * e_neg)), stack(bf(ke * e_neg))], axis=0),
            bkd=jnp.concatenate([bf(b * e_tail), bf(ke * e_tail)], axis=0),
            e_last_col=jnp.transpose(jnp.broadcast_to(jnp.exp(g_last), (LANES, LANES)))))
    yield

    sc = [_dg(jnp.concatenate([q["ag"], q["rg"]], axis=0), q["bk_s"], nt) for q in pre]
    n_ab_f = [jnp.where(strict, t[:n, :LANES], 0.0) for t in sc]
    n_ab = [bf(t) for t in n_ab_f]
    nr_k = [bf(jnp.concatenate([jnp.where(strict, t[:n, LANES:], 0.0),
                                jnp.where(incl, t[n:, LANES:], 0.0)], axis=0)) for t in sc]
    a_rb = [bf(jnp.where(incl, t[n:, :LANES], 0.0)) for t in sc]
    yield

    kv = [_dg(nr_k[i], stack(pre[i]["vb"]), nn) for i in rng]
    yield

    pk = n_ab
    t = [eye + n_ab_f[i] for i in rng]
    p_next = [bf(_dg(pk[i], stack(pk[i]), nn)) for i in rng]
    yield
    for step in range(1, 6):
        pk = p_next
        if step < 5:
            m = [_dg(pk[i], jnp.concatenate([stack(bf(t[i])), stack(pk[i])], axis=1), nn)
                 for i in rng]
            p_next = [bf(m[i][:, LANES:]) for i in rng]
            t = [t[i] + m[i][:, :LANES] for i in rng]
        else:
            t = [t[i] + _dg(pk[i], stack(bf(t[i])), nn) for i in rng]
        yield

    x = [_dg(bf(t[i]), jnp.concatenate([stack(pre[i]["ag"]), stack(bf(kv[i][:n]))], axis=1), nn)
         for i in rng]
    yield

    sts = [state[p] for p in pairs]
    stb = [bf(s_) for s_ in sts]
    u = [_dg(bf(x[i][:, :LANES]), stb[i], nn) + x[i][:, LANES:] for i in rng]
    ub = [bf(u_) for u_ in u]
    yield

    for i, p in enumerate(pairs):
        upd = _dg(pre[i]["bkd"], jnp.concatenate([ub[i], pre[i]["vb"]], axis=0), tn)
        state[p] = pre[i]["e_last_col"] * sts[i] + jnp.where(rowhead, upd, 0.0)
    y = [_dg(jnp.concatenate([pre[i]["rg"], a_rb[i]], axis=1),
             jnp.concatenate([stb[i], stack(ub[i])], axis=0), nn) + kv[i][n:] for i in rng]
    yield

    for i, p in enumerate(pairs):
        q = pre[i]
        dlt = y[i] - head_sum(y[i]) * (1.0 / RW_HD)
        var = head_sum(dlt * dlt) * (1.0 / RW_HD)
        yn = dlt * lax.rsqrt(var + RW_GN_EPS)
        emit(p, yn * q["lnw"] + q["lnb"] + q["bonus"] * q["v"])


def _rwkv_body(zr_ref, zk_ref, zv_ref, zl_ref, gr_ref, pr_ref, mul_ref, wl_ref, o_ref,
               carry, st_ref):
    n = CHUNK

    @pl.when(pl.program_id(1) == 0)
    def _():
        carry[...] = jnp.zeros_like(carry)
        st_ref[...] = jnp.zeros_like(st_ref)

    lane = _iota((1, LANES), 1)
    m_top = lane < RW_HD
    r2 = _iota((2 * n, LANES), 0)
    c2 = _iota((2 * n, LANES), 1)
    rowhead = (r2 < n) == (c2 < RW_HD)
    rw = _iota((n, LANES), 0)
    sw = jnp.bitwise_and(_iota((n, LANES), 1), RW_HD - 1)
    strict = sw < rw
    incl = sw <= rw
    eye = jnp.where(sw == rw, 1.0, 0.0)
    masks = (m_top, rowhead, strict, incl, eye)
    tri = _tri_incl(n)
    first_row = _iota((n, 1), 0) == 0

    def shift_mix(z, prev_row, mu):
        prev = jnp.where(first_row, prev_row, pltpu.roll(z, 1, 0))
        return z + (prev - z) * mu

    def body(it, _):
        state = {p: st_ref[p] for p in range(RW_PAIRS)}
        last = dict(r=carry[0:1, 0:D], k=carry[0:1, D:2 * D], v=carry[0:1, 2 * D:3 * D],
                    l=carry[0:1, 3 * D:3 * D + LANES])
        chunk_vals = {}

        def chunk_level(j):
            if j in chunk_vals:
                return chunk_vals[j]
            rows = pl.ds(pl.multiple_of((it * RW_UNROLL + j) * n, n), n)
            zr = zr_ref[rows, :]
            zk = zk_ref[rows, :]
            zv = zv_ref[rows, :]
            zl = zl_ref[rows, :]
            r = shift_mix(zr, last["r"], pr_ref[_P_MUR:_P_MUR + 1, :])
            k = shift_mix(zk, last["k"], pr_ref[_P_MUK:_P_MUK + 1, :])
            v = shift_mix(zv, last["v"], pr_ref[_P_MUV:_P_MUV + 1, :])
            lo = shift_mix(zl, last["l"], mul_ref[...])
            last.update(r=zr[n - 1:n, :], k=zk[n - 1:n, :], v=zv[n - 1:n, :], l=zl[n - 1:n, :])
            lo = jnp.where(_iota((n, LANES), 1) < 64, jnp.tanh(lo), lo)
            up = _dg(lo.astype(BF16), wl_ref[...], ((1,), (0,)))
            lw = -DECAY_SCALE * _sigmoid(pr_ref[_P_W0:_P_W0 + 1, :] + up[:, :D])
            a = _sigmoid(pr_ref[_P_A0:_P_A0 + 1, :] + up[:, D:])
            g = _cumsum_rows(tri, lw)
            ke = k * (1.0 + (a - 1.0) * pr_ref[_P_KA:_P_KA + 1, :])
            kkraw = k * pr_ref[_P_KK:_P_KK + 1, :]
            chunk_vals[j] = (rows, (r, ke, v, lw, g, a, kkraw))
            return chunk_vals[j]

        def make_task(j, pairs):
            def get_ops():
                _, arrs = chunk_level(j)
                out = []
                for p in pairs:
                    ln = slice(p * LANES, (p + 1) * LANES)
                    out.append(tuple(t[:, ln] for t in arrs) + (
                        pr_ref[_P_RK:_P_RK + 1, ln], pr_ref[_P_LNW:_P_LNW + 1, ln],
                        pr_ref[_P_LNB:_P_LNB + 1, ln]))
                return out

            def emit(p, y):
                rows, _ = chunk_level(j)
                ln = slice(p * LANES, (p + 1) * LANES)
                o_ref[rows, ln] = (y * _silu(gr_ref[rows, ln])).astype(o_ref.dtype)

            return _rwkv_task(get_ops, pairs, state, masks, emit)

        per_group = RW_PAIRS // RW_GROUPS
        groups = [list(range(gi * per_group, (gi + 1) * per_group)) for gi in range(RW_GROUPS)]
        _run_skewed([make_task(j, grp) for j in range(RW_UNROLL) for grp in groups], RW_SKEW)

        for p in range(RW_PAIRS):
            st_ref[p] = state[p]
        carry[0:1, 0:D] = last["r"]
        carry[0:1, D:2 * D] = last["k"]
        carry[0:1, 2 * D:3 * D] = last["v"]
        carry[0:1, 3 * D:3 * D + LANES] = last["l"]
        return 0

    lax.fori_loop(0, TS // (n * RW_UNROLL), body, 0)


def _rwkv(zr, zk, zv, zl, gr, pr, mul, wl):
    bsz, s, _ = zr.shape
    seq = lambda w: pl.BlockSpec((None, TS, w), lambda b, i: (b, i, 0))
    return pl.pallas_call(
        _rwkv_body,
        grid=(bsz, s // TS),
        in_specs=[seq(D), seq(D), seq(D), seq(LANES), seq(D),
                  _const_spec(pr.shape), _const_spec(mul.shape), _const_spec(wl.shape)],
        out_specs=seq(D),
        out_shape=jax.ShapeDtypeStruct((bsz, s, D), BF16),
        scratch_shapes=[pltpu.VMEM((8, 3 * D + LANES), F32),
                        pltpu.VMEM((RW_PAIRS, LANES, LANES), F32)],
        compiler_params=_params(("parallel", "arbitrary")),
        name="rwkv",
    )(zr, zk, zv, zl, gr, pr, mul, wl)


def _log_sigmoid(x):
    return jnp.minimum(x, 0.0) - jnp.log(1.0 + jnp.exp(-jnp.abs(x)))


def _mlstm_task(rows, refs, state, consts):
    (xm_ref, gm_ref, cw_ref, cb_ref, wq_ref, wk_ref, wv_ref, wif_ref, wift_ref, bif_ref,
     bift_ref, nrm_ref, skp_ref, o_ref) = refs
    tri, causal = consts
    n = CHUNK
    nn, nt, tn = ((1,), (0,)), ((1,), (1,)), ((0,), (0,))
    hs = range(ML_HEADS)
    lns = [slice(h * ML_HD, (h + 1) * ML_HD) for h in hs]
    k_scale = ML_HD ** -0.5

    xm = xm_ref[rows, :]
    xc = _silu(_conv4(xm, state["prev8"], cw_ref, cb_ref))
    state["prev8"] = xm[n - 8:, :]
    yield

    xcb = xc.astype(BF16)
    xmb = xm.astype(BF16)
    qb = [_dg(xcb[:, ln], wq_ref[h], nn).astype(BF16) for h, ln in enumerate(lns)]
    kf = [_dg(xcb[:, ln], wk_ref[h], nn) for h, ln in enumerate(lns)]
    vf = [_dg(xmb[:, ln], wv_ref[h], nn) for h, ln in enumerate(lns)]
    kb = [t.astype(BF16) for t in kf]
    vb = [t.astype(BF16) for t in vf]
    yield

    gc = bif_ref[...]
    gt = bift_ref[...]
    for j, ts in enumerate((qb, kb, vb)):
        for h in hs:
            gc = gc + _dg(ts[h], wif_ref[j, lns[h], :], nn)
            gt = gt + _dg(wift_ref[j, :, lns[h]], ts[h], nt)
    bc_c = _mm_exact_lhs(tri, _log_sigmoid(gc))
    bc_t = _mm_exact_rhs(_log_sigmoid(gt), tri, nt)
    li_c = [gc[:, h:h + 1] for h in hs]
    li_t = [gt[h:h + 1, :] for h in hs]
    b_c = [bc_c[:, ML_HEADS + h:ML_HEADS + h + 1] for h in hs]
    b_t = [bc_t[ML_HEADS + h:ML_HEADS + h + 1, :] for h in hs]
    qk = [_dg(qb[h], kb[h], nt) for h in hs]
    log_d = [jnp.where(causal, b_c[h] - b_t[h] + li_t[h], -jnp.inf) for h in hs]
    b_last = [b_c[h][n - 1:n, :] for h in hs]
    log_g = [b_last[h] - b_c[h] + li_c[h] for h in hs]
    yield

    m_prev = [state["m", h] for h in hs]
    ct = [state["ct", h] for h in hs]
    nv = [state["nv", h] for h in hs]
    qc = [_dg(qb[h], ct[h].astype(BF16), nn) for h in hs]
    log_inter = [b_c[h] + m_prev[h] for h in hs]
    m_t = [jnp.maximum(jnp.max(log_d[h], axis=-1, keepdims=True), log_inter[h]) for h in hs]
    s = [qk[h] * k_scale * jnp.exp(log_d[h] - m_t[h]) for h in hs]
    w_inter = [jnp.exp(log_inter[h] - m_t[h]) for h in hs]
    m_new = [jnp.maximum(b_last[h] + m_prev[h], jnp.max(log_g[h], axis=0, keepdims=True))
             for h in hs]
    gcol = [jnp.exp(log_g[h] - m_new[h]) for h in hs]
    decay = [jnp.exp(b_last[h] + m_prev[h] - m_new[h]) for h in hs]
    yield

    sv = [_dg(s[h].astype(BF16), vb[h], nn) for h in hs]
    kgv = [_dg(kb[h], (gcol[h] * vf[h]).astype(BF16), tn) for h in hs]
    yield

    for h in hs:
        state["ct", h] = decay[h] * ct[h] + kgv[h] * k_scale
        state["nv", h] = decay[h] * nv[h] + k_scale * jnp.sum(gcol[h] * kf[h], axis=0,
                                                              keepdims=True)
        state["m", h] = m_new[h]
    yield

    for h in hs:
        ln = lns[h]
        num = sv[h] + w_inter[h] * qc[h]
        den = (jnp.sum(s[h], axis=-1, keepdims=True)
               + w_inter[h] * jnp.sum(qb[h].astype(F32) * nv[h], axis=-1, keepdims=True))
        hh = num / (jnp.maximum(jnp.abs(den), jnp.exp(-m_t[h])) + ML_EPS)
        mu = jnp.mean(hh, axis=-1, keepdims=True)
        dlt = hh - mu
        var = jnp.mean(dlt * dlt, axis=-1, keepdims=True)
        hn = dlt * lax.rsqrt(var + ML_LN_EPS) * nrm_ref[:, ln]
        ym = hn + skp_ref[:, ln] * xc[:, ln]
        o_ref[rows, ln] = (ym * _silu(gm_ref[rows, ln])).astype(o_ref.dtype)


def _mlstm_body(xm_ref, gm_ref, cw_ref, cb_ref, wq_ref, wk_ref, wv_ref, wif_ref, wift_ref,
                bif_ref, bift_ref, nrm_ref, skp_ref, o_ref, prev8, ct_ref, n_ref, m_ref):
    n = CHUNK

    @pl.when(pl.program_id(1) == 0)
    def _():
        prev8[...] = jnp.zeros_like(prev8)
        ct_ref[...] = jnp.zeros_like(ct_ref)
        n_ref[...] = jnp.zeros_like(n_ref)
        m_ref[...] = jnp.zeros_like(m_ref)

    consts = (_tri_incl(n), _iota((n, n), 1) <= _iota((n, n), 0))
    refs = (xm_ref, gm_ref, cw_ref, cb_ref, wq_ref, wk_ref, wv_ref, wif_ref, wift_ref, bif_ref,
            bift_ref, nrm_ref, skp_ref, o_ref)

    def body(it, _):
        state = {"prev8": prev8[...]}
        for h in range(ML_HEADS):
            state["ct", h] = ct_ref[h]
            state["nv", h] = n_ref[0:1, h * ML_HD:(h + 1) * ML_HD]
            state["m", h] = m_ref[h:h + 1, 0:1]
        tasks = [_mlstm_task(pl.ds(pl.multiple_of((it * ML_UNROLL + j) * n, n), n), refs, state,
                             consts) for j in range(ML_UNROLL)]
        _run_skewed(tasks, ML_SKEW)
        prev8[...] = state["prev8"]
        for h in range(ML_HEADS):
            ct_ref[h] = state["ct", h]
            n_ref[0:1, h * ML_HD:(h + 1) * ML_HD] = state["nv", h]
            m_ref[h:h + 1, :] = jnp.broadcast_to(state["m", h], (1, LANES))
        return 0

    lax.fori_loop(0, TS // (n * ML_UNROLL), body, 0)


def _mlstm(xm, gm, cw, cb, wq, wk, wv, wif, wift, bif, bift, nrm, skp):
    bsz, s, _ = xm.shape
    seq = pl.BlockSpec((None, TS, D), lambda b, i: (b, i, 0))
    consts = (cw, cb, wq, wk, wv, wif, wift, bif, bift, nrm, skp)
    return pl.pallas_call(
        _mlstm_body,
        grid=(bsz, s // TS),
        in_specs=[seq, seq] + [_const_spec(a.shape) for a in consts],
        out_specs=seq,
        out_shape=jax.ShapeDtypeStruct((bsz, s, D), BF16),
        scratch_shapes=[pltpu.VMEM((8, D), F32),
                        pltpu.VMEM((ML_HEADS, ML_HD, ML_HD), F32),
                        pltpu.VMEM((8, D), F32),
                        pltpu.VMEM((8, LANES), F32)],
        compiler_params=_params(("parallel", "arbitrary")),
        name="mlstm",
    )(xm, gm, *consts)


def _mix_out(h_ref, ya_ref, yb_ref, wo_ref, p_ref, pn_ref, pu_ref, pg_ref):
    h = h_ref[...] + _dg(ya_ref[...], wo_ref[:D, :], ((1,), (0,)))
    h = h + _dg(yb_ref[...], wo_ref[D:, :], ((1,), (0,)))
    hn = _rms(h, pn_ref[...]).astype(BF16)
    gate = _sigmoid(_dg(hn, pg_ref[...], ((1,), (0,))))
    return h + _dg(p_ref[...].astype(BF16), pu_ref[...], ((1,), (0,))) * gate


def _mid_body(h_ref, ya_ref, yb_ref, wo_ref, p_ref, pn_ref, pu_ref, pg_ref, mn_ref, wi_ref,
              ho_ref, xl_ref, g_ref):
    h = _mix_out(h_ref, ya_ref, yb_ref, wo_ref, p_ref, pn_ref, pu_ref, pg_ref)
    ho_ref[...] = h
    xn = _rms(h, mn_ref[...]).astype(BF16)
    xl_ref[...] = _dg(xn, wi_ref[:, :LRU_W], ((1,), (0,)))
    g_ref[...] = _dg(xn, wi_ref[:, LRU_W:], ((1,), (0,)))


def _final_body(h_ref, ya_ref, yb_ref, wo_ref, p_ref, pn_ref, pu_ref, pg_ref, fn_ref, o_ref):
    h = _mix_out(h_ref, ya_ref, yb_ref, wo_ref, p_ref, pn_ref, pu_ref, pg_ref)
    o_ref[...] = _rms(h, fn_ref[...])


def _row_spec(w, half=None):
    if half is None:
        return pl.BlockSpec((TM, w), lambda i: (i, 0))
    return pl.BlockSpec((TM, w), lambda i: (i, half))


def _mid(h, ya, yb, wo, p, pn, pu, pg, mn, wi):
    t = h.shape[0]
    return pl.pallas_call(
        _mid_body,
        grid=(t // TM,),
        in_specs=[_row_spec(D), _row_spec(D), _row_spec(D), _const_spec(wo.shape),
                  _row_spec(PLE), _const_spec(pn.shape), _const_spec(pu.shape),
                  _const_spec(pg.shape), _const_spec(mn.shape), _const_spec(wi.shape)],
        out_specs=[_row_spec(D), _row_spec(LRU_W), _row_spec(LRU_W)],
        out_shape=[jax.ShapeDtypeStruct((t, D), F32), jax.ShapeDtypeStruct((t, LRU_W), F32),
                   jax.ShapeDtypeStruct((t, LRU_W), F32)],
        compiler_params=_params(("parallel",)),
        name="mid",
    )(h, ya, yb, wo, p, pn, pu, pg, mn, wi)


def _final(h, y, wo, p, pn, pu, pg, fn):
    t = h.shape[0]
    return pl.pallas_call(
        _final_body,
        grid=(t // TM,),
        in_specs=[_row_spec(D), _row_spec(D, 0), _row_spec(D, 1), _const_spec(wo.shape),
                  _row_spec(PLE), _const_spec(pn.shape), _const_spec(pu.shape),
                  _const_spec(pg.shape), _const_spec(fn.shape)],
        out_specs=_row_spec(D),
        out_shape=jax.ShapeDtypeStruct((t, D), F32),
        compiler_params=_params(("parallel",)),
        name="final",
    )(h, y, y, wo, p, pn, pu, pg, fn)


def _rglru_body(xl_ref, g_ref, cw_ref, cb_ref, wr_ref, br_ref, wi_ref, bi_ref, lam_ref, o_ref,
                prev8, hc):
    n = CHUNK
    w = LRU_STRIP

    @pl.when(pl.program_id(2) == 0)
    def _():
        prev8[...] = jnp.zeros_like(prev8)
        hc[...] = jnp.zeros_like(hc)

    row8 = _iota((n // 8, 8, w), 1)

    def chunk_body(c, _):
        rows = pl.ds(pl.multiple_of(c * n, n), n)
        xl = xl_ref[rows, :]
        xc = _conv4(xl, prev8[...], cw_ref, cb_ref)
        prev8[...] = xl[n - 8:, :]
        rg = []
        ig = []
        for j in range(w // LRU_BLK):
            xb = xc[:, j * LRU_BLK:(j + 1) * LRU_BLK].astype(BF16)
            rg.append(_dg(xb, wr_ref[j], ((1,), (0,))))
            ig.append(_dg(xb, wi_ref[j], ((1,), (0,))))
        rg = _sigmoid(jnp.concatenate(rg, axis=1) + br_ref[...])
        ig = _sigmoid(jnp.concatenate(ig, axis=1) + bi_ref[...])
        log_a = -LRU_C * rg * _softplus(-lam_ref[...])
        a = jnp.exp(log_a)
        mult = jnp.sqrt(jnp.maximum(1.0 - jnp.exp(2.0 * log_a), 0.0))
        u = xc * ig * mult
        u = u.reshape(n // 8, 8, w)
        a = a.reshape(n // 8, 8, w)
        for d in (1, 2, 4):
            keep = row8 >= d
            u = u + a * jnp.where(keep, pltpu.roll(u, d, 1), 0.0)
            a = a * jnp.where(keep, pltpu.roll(a, d, 1), 1.0)
        carry = hc[0:1, :]
        hs = []
        for j in range(n // 8):
            hj = u[j] + a[j] * carry
            carry = hj[7:8, :]
            hs.append(hj)
        hc[0:1, :] = carry
        h = jnp.concatenate(hs, axis=0)
        o_ref[rows, :] = (h * _silu(g_ref[rows, :])).astype(o_ref.dtype)
        return 0

    lax.fori_loop(0, TS // n, chunk_body, 0, unroll=LRU_UNROLL)


def _rglru(xl, g, cw, cb, wr, br, wi, bi, lam):
    bsz, s, _ = xl.shape
    ns = LRU_W // LRU_STRIP
    seq = pl.BlockSpec((None, TS, LRU_STRIP), lambda b, j, i: (b, i, j))
    vec = lambda r: pl.BlockSpec((r, LRU_STRIP), lambda b, j, i: (0, j))
    blk = pl.BlockSpec((LRU_STRIP // LRU_BLK, LRU_BLK, LRU_BLK), lambda b, j, i: (j, 0, 0))
    return pl.pallas_call(
        _rglru_body,
        grid=(bsz, ns, s // TS),
        in_specs=[seq, seq, vec(CONV_W), vec(1), blk, vec(1), blk, vec(1), vec(1)],
        out_specs=seq,
        out_shape=jax.ShapeDtypeStruct((bsz, s, LRU_W), BF16),
        scratch_shapes=[pltpu.VMEM((8, LRU_STRIP), F32), pltpu.VMEM((8, LRU_STRIP), F32)],
        compiler_params=_params(("parallel", "parallel", "arbitrary")),
        name="rglru",
    )(xl, g, cw, cb, wr, br, wi, bi, lam)


def _block4_tiles(w):
    nblk = w.shape[0]
    per = ML_HD // QKV_BLOCK
    eye = jnp.eye(per, dtype=w.dtype)
    wt = w.reshape(nblk // per, per, QKV_BLOCK, QKV_BLOCK)
    dense = jnp.einsum("tgio,gh->tgiho", wt, eye)
    return dense.reshape(nblk // per, ML_HD, ML_HD)


def kernel(x, p, mix_norm, pe_norm, final_norm, pe_up, pe_gate, ab_w_in, rwkv_mu, rwkv_mu_lora, rwkv_w0, rwkv_w_up, rwkv_a0, rwkv_a_up, rwkv_k_k, rwkv_k_a, rwkv_r_k, rwkv_ln_w, rwkv_ln_b, mlstm_conv_w, mlstm_conv_b, mlstm_wq, mlstm_wk, mlstm_wv, mlstm_w_if, mlstm_b_if, mlstm_norm, mlstm_skip, ab_w_out, c_w_in, c_conv_w, c_conv_b, c_wr, c_br, c_wi, c_bi, c_lambda, c_w_out):
    bsz, s, _ = x.shape
    t = bsz * s
    row = lambda a: a.reshape(1, -1)
    seq = lambda a: a.reshape(bsz, s, a.shape[-1])

    w = ab_w_in[0]
    w_in0 = jnp.concatenate([w[:, :3 * D], w[:, 3 * D + 128:], w[:, 3 * D:3 * D + 128]],
                            axis=1).astype(BF16)
    zr, zk, zv, xm, gr, gm, zl = _proj0(x.reshape(t, D), row(mix_norm[0]), w_in0)

    pr = jnp.concatenate([rwkv_mu[0], row(rwkv_w0[0]), row(rwkv_a0[0]), row(rwkv_k_k[0]),
                          row(rwkv_k_a[0]), row(rwkv_r_k[0]), row(rwkv_ln_w[0]),
                          row(rwkv_ln_b[0]), jnp.zeros((6, D), F32)], axis=0)
    mul = rwkv_mu_lora[0].reshape(1, 128)
    zeros = jnp.zeros((64, D), F32)
    wl = jnp.concatenate([jnp.concatenate([rwkv_w_up[0], zeros], axis=1),
                          jnp.concatenate([zeros, rwkv_a_up[0]], axis=1)], axis=0)
    y_rwkv = _rwkv(seq(zr), seq(zk), seq(zv), seq(zl), seq(gr), pr, mul, wl.astype(BF16))

    wif = mlstm_w_if[0].reshape(3, D, 2 * ML_HEADS)
    y_m = _mlstm(seq(xm), seq(gm), mlstm_conv_w[0], row(mlstm_conv_b[0]),
                 _block4_tiles(mlstm_wq[0]).astype(BF16), _block4_tiles(mlstm_wk[0]).astype(BF16),
                 _block4_tiles(mlstm_wv[0]).astype(BF16), wif.astype(BF16),
                 jnp.swapaxes(wif, 1, 2).astype(BF16),
                 row(mlstm_b_if[0]), mlstm_b_if[0].reshape(-1, 1),
                 row(mlstm_norm[0]), row(mlstm_skip[0]))

    h1, xl, g1 = _mid(x.reshape(t, D), y_rwkv.reshape(t, D), y_m.reshape(t, D),
                      ab_w_out[0].astype(BF16), p[0].reshape(t, PLE), row(pe_norm[0]),
                      pe_up[0].astype(BF16), pe_gate[0].astype(BF16), row(mix_norm[1]),
                      c_w_in[0].astype(BF16))

    y_lru = _rglru(seq(xl), seq(g1), c_conv_w[0], row(c_conv_b[0]), c_wr[0].astype(BF16),
                   row(c_br[0]), c_wi[0].astype(BF16), row(c_bi[0]), row(c_lambda[0]))
    out = _final(h1, y_lru.reshape(t, LRU_W), c_w_out[0].astype(BF16), p[1].reshape(t, PLE),
                 row(pe_norm[1]), pe_up[1].astype(BF16), pe_gate[1].astype(BF16),
                 row(final_norm))
    return out.reshape(bsz, s, D)
```

```python
import jax
import jax.numpy as jnp
from jax import lax
from jax.experimental import pallas as pl
from jax.experimental.pallas import tpu as pltpu

F32 = jnp.float32
BF16 = jnp.bfloat16

D = 1024
LANES = 128
CHUNK = 64
NORM_EPS = 1e-6
PLE = 256
RW_HEADS = 16
RW_HD = 64
RW_PAIRS = RW_HEADS // 2
DECAY_SCALE = 0.6065306597126334
RW_GN_EPS = 64e-5
ML_HEADS = 4
ML_HD = 256
ML_EPS = 1e-6
ML_LN_EPS = 1e-5
QKV_BLOCK = 4
CONV_W = 4
LRU_W = 2 * D
LRU_BLK = 128
LRU_C = 8.0
AB_IN = 4 * D + 128 + 2 * D

TM = 512
TS = 512
LRU_TS = 1024
LRU_STRIP = 1024
RW_UNROLL = 4
RW_GROUPS = 1
RW_SKEW = 2
ML_UNROLL = 8
ML_SKEW = 2
LRU_UNROLL = 4
VMEM_LIMIT = 56 * 1024 * 1024

assert CHUNK == RW_HD and ML_SKEW >= 2 and RW_GROUPS * RW_SKEW >= 1


def _dg(a, b, dims):
    return lax.dot_general(a, b, (dims, ((), ())), preferred_element_type=F32)


def _split2(x):
    hi = x.astype(BF16)
    lo = (x - hi.astype(F32)).astype(BF16)
    return hi, lo


def _split3(x):
    hi = x.astype(BF16)
    r = x - hi.astype(F32)
    mid = r.astype(BF16)
    lo = (r - mid.astype(F32)).astype(BF16)
    return hi, mid, lo


def _cumsum_rows(tri_bf16, x):
    hi, lo = _split2(x)
    return _dg(tri_bf16, hi, ((1,), (0,))) + _dg(tri_bf16, lo, ((1,), (0,)))


def _mm_exact_lhs(a_bf16, b, dims=((1,), (0,))):
    h, m, l = _split3(b)
    return _dg(a_bf16, h, dims) + (_dg(a_bf16, m, dims) + _dg(a_bf16, l, dims))


def _mm_exact_rhs(a, b_bf16, dims=((1,), (0,))):
    h, m, l = _split3(a)
    return _dg(h, b_bf16, dims) + (_dg(m, b_bf16, dims) + _dg(l, b_bf16, dims))


def _rms(x, g):
    return x * lax.rsqrt(jnp.mean(x * x, axis=-1, keepdims=True) + NORM_EPS) * g


def _sigmoid(x):
    return 1.0 / (1.0 + jnp.exp(-x))


def _silu(x):
    return x * _sigmoid(x)


def _softplus(x):
    return jnp.maximum(x, 0.0) + jnp.log(1.0 + jnp.exp(-jnp.abs(x)))


def _iota(shape, axis):
    return lax.broadcasted_iota(jnp.int32, shape, axis)


def _tri_incl(n):
    return jnp.where(_iota((n, n), 1) <= _iota((n, n), 0), 1.0, 0.0).astype(BF16)


def _shift_rows(x, d, prev8):
    xs = pltpu.roll(x, d, 0)
    ps = pltpu.roll(prev8, d, 0)
    head = jnp.where(_iota((8, x.shape[1]), 0) < d, ps, xs[:8])
    return jnp.concatenate([head, xs[8:]], axis=0)


def _conv4(x, prev8, w_ref, b_ref):
    acc = x * w_ref[3:4, :] + b_ref[...]
    for d in (1, 2, 3):
        acc = acc + _shift_rows(x, d, prev8) * w_ref[3 - d:4 - d, :]
    return acc


def _run_skewed(tasks, skew):
    pending = list(tasks)
    active = []
    tick = 0
    while pending or active:
        if pending and tick % skew == 0:
            active.append(pending.pop(0))
        for t in list(active):
            try:
                next(t)
            except StopIteration:
                active.remove(t)
        tick += 1


def _const_spec(shape):
    nd = len(shape)
    return pl.BlockSpec(shape, lambda *_: (0,) * nd, pipeline_mode=pl.Buffered(1))


def _params(sem):
    return pltpu.CompilerParams(dimension_semantics=sem, vmem_limit_bytes=VMEM_LIMIT)


def _proj0_body(x_ref, g_ref, w_ref, zr, zk, zv, xm, gr, gm, zl):
    xn = _rms(x_ref[...], g_ref[...]).astype(BF16)
    for j, o in enumerate((zr, zk, zv)):
        o[...] = _dg(xn, w_ref[:, j * D:(j + 1) * D], ((1,), (0,)))
    zl[...] = _dg(xn, w_ref[:, 3 * D:3 * D + 128], ((1,), (0,)))
    for j, o in enumerate((xm, gr, gm)):
        o[...] = _dg(xn, w_ref[:, 3 * D + 128 + j * D:3 * D + 128 + (j + 1) * D], ((1,), (0,)))


def _proj0(x2, g, w):
    t = x2.shape[0]
    wide = jax.ShapeDtypeStruct((t, D), F32)
    row = pl.BlockSpec((TM, D), lambda i: (i, 0))
    return pl.pallas_call(
        _proj0_body,
        grid=(t // TM,),
        in_specs=[row, _const_spec((1, D)), _const_spec((D, AB_IN))],
        out_specs=[row] * 6 + [pl.BlockSpec((TM, 128), lambda i: (i, 0))],
        out_shape=[wide] * 6 + [jax.ShapeDtypeStruct((t, 128), F32)],
        compiler_params=_params(("parallel",)),
        name="proj0",
    )(x2, g, w)


_P_MUR, _P_MUK, _P_MUV, _P_W0, _P_A0, _P_KK, _P_KA, _P_RK, _P_LNW, _P_LNB = range(10)


def _rwkv_task(get_ops, pairs, state, masks, emit):
    m_top, rowhead, strict, incl, eye = masks
    n = CHUNK
    rng = range(len(pairs))
    bf = lambda t: t.astype(BF16)
    nn, nt, tn = ((1,), (0,)), ((1,), (1,)), ((0,), (0,))

    def stack(x):
        z = jnp.zeros_like(x)
        return jnp.concatenate([jnp.where(m_top, x, z), jnp.where(m_top, z, x)], axis=0)

    def unstack_col(c):
        return jnp.where(m_top, c[:n], c[n:])

    def head_sum(x):
        return unstack_col(jnp.sum(stack(x), axis=-1, keepdims=True))

    pre = []
    for r, ke, v, lw, g, a, kkraw, rk, lnw, lnb in get_ops():
        kk = kkraw / jnp.maximum(jnp.sqrt(head_sum(jnp.square(kkraw))), 1e-12)
        bonus = head_sum(r * ke * rk)
        g_last = g[n - 1:n, :]
        b = kk * a
        e_neg = jnp.exp(-g)
        e_tail = jnp.exp(g_last - g)
        ag = bf(-kk * jnp.exp(g - lw))
        vb = bf(v)
        pre.append(dict(
            bonus=bonus, v=v, lnw=lnw, lnb=lnb, vb=vb, ag=ag, rg=bf(r * jnp.exp(g)),
            bk_s=jnp.concatenate([stack(bf(b * e_neg)), stack(bf(ke * e_neg))], axis=0),
            bkd=jnp.concatenate([bf(b * e_tail), bf(ke * e_tail)], axis=0),
            e_last_col=jnp.transpose(jnp.broadcast_to(jnp.exp(g_last), (LANES, LANES)))))
    yield

    sc = [_dg(jnp.concatenate([q["ag"], q["rg"]], axis=0), q["bk_s"], nt) for q in pre]
    n_ab_f = [jnp.where(strict, t[:n, :LANES], 0.0) for t in sc]
    n_ab = [bf(t) for t in n_ab_f]
    nr_k = [bf(jnp.concatenate([jnp.where(strict, t[:n, LANES:], 0.0),
                                jnp.where(incl, t[n:, LANES:], 0.0)], axis=0)) for t in sc]
    a_rb = [bf(jnp.where(incl, t[n:, :LANES], 0.0)) for t in sc]
    yield

    kv = [_dg(nr_k[i], stack(pre[i]["vb"]), nn) for i in rng]
    yield

    pk = n_ab
    t = [eye + n_ab_f[i] for i in rng]
    p_next = [bf(_dg(pk[i], stack(pk[i]), nn)) for i in rng]
    yield
    for step in range(1, 6):
        pk = p_next
        if step < 5:
            m = [_dg(pk[i], jnp.concatenate([stack(bf(t[i])), stack(pk[i])], axis=1), nn)
                 for i in rng]
            p_next = [bf(m[i][:, LANES:]) for i in rng]
            t = [t[i] + m[i][:, :LANES] for i in rng]
        else:
            t = [t[i] + _dg(pk[i], stack(bf(t[i])), nn) for i in rng]
        yield

    x = [_dg(bf(t[i]), jnp.concatenate([stack(pre[i]["ag"]), stack(bf(kv[i][:n]))], axis=1), nn)
         for i in rng]
    yield

    sts = [state[p] for p in pairs]
    stb = [bf(s_) for s_ in sts]
    u = [_dg(bf(x[i][:, :LANES]), stb[i], nn) + x[i][:, LANES:] for i in rng]
    ub = [bf(u_) for u_ in u]
    yield

    for i, p in enumerate(pairs):
        upd = _dg(pre[i]["bkd"], jnp.concatenate([ub[i], pre[i]["vb"]], axis=0), tn)
        state[p] = pre[i]["e_last_col"] * sts[i] + jnp.where(rowhead, upd, 0.0)
    y = [_dg(jnp.concatenate([pre[i]["rg"], a_rb[i]], axis=1),
             jnp.concatenate([stb[i], stack(ub[i])], axis=0), nn) + kv[i][n:] for i in rng]
    yield

    for i, p in enumerate(pairs):
        q = pre[i]
        dlt = y[i] - head_sum(y[i]) * (1.0 / RW_HD)
        var = head_sum(dlt * dlt) * (1.0 / RW_HD)
        yn = dlt * lax.rsqrt(var + RW_GN_EPS)
        emit(p, yn * q["lnw"] + q["lnb"] + q["bonus"] * q["v"])


def _rwkv_body(zr_ref, zk_ref, zv_ref, zl_ref, gr_ref, pr_ref, mul_ref, wl_ref, o_ref,
               carry, st_ref):
    n = CHUNK

    @pl.when(pl.program_id(1) == 0)
    def _():
        carry[...] = jnp.zeros_like(carry)
        st_ref[...] = jnp.zeros_like(st_ref)

    lane = _iota((1, LANES), 1)
    m_top = lane < RW_HD
    r2 = _iota((2 * n, LANES), 0)
    c2 = _iota((2 * n, LANES), 1)
    rowhead = (r2 < n) == (c2 < RW_HD)
    rw = _iota((n, LANES), 0)
    sw = jnp.bitwise_and(_iota((n, LANES), 1), RW_HD - 1)
    strict = sw < rw
    incl = sw <= rw
    eye = jnp.where(sw == rw, 1.0, 0.0)
    masks = (m_top, rowhead, strict, incl, eye)
    tri = _tri_incl(n)
    first_row = _iota((n, 1), 0) == 0

    def shift_mix(z, prev_row, mu):
        prev = jnp.where(first_row, prev_row, pltpu.roll(z, 1, 0))
        return z + (prev - z) * mu

    def body(it, _):
        state = {p: st_ref[p] for p in range(RW_PAIRS)}
        last = dict(r=carry[0:1, 0:D], k=carry[0:1, D:2 * D], v=carry[0:1, 2 * D:3 * D],
                    l=carry[0:1, 3 * D:3 * D + LANES])
        chunk_vals = {}

        def chunk_level(j):
            if j in chunk_vals:
                return chunk_vals[j]
            rows = pl.ds(pl.multiple_of((it * RW_UNROLL + j) * n, n), n)
            zr = zr_ref[rows, :]
            zk = zk_ref[rows, :]
            zv = zv_ref[rows, :]
            zl = zl_ref[rows, :]
            r = shift_mix(zr, last["r"], pr_ref[_P_MUR:_P_MUR + 1, :])
            k = shift_mix(zk, last["k"], pr_ref[_P_MUK:_P_MUK + 1, :])
            v = shift_mix(zv, last["v"], pr_ref[_P_MUV:_P_MUV + 1, :])
            lo = shift_mix(zl, last["l"], mul_ref[...])
            last.update(r=zr[n - 1:n, :], k=zk[n - 1:n, :], v=zv[n - 1:n, :], l=zl[n - 1:n, :])
            lo = jnp.where(_iota((n, LANES), 1) < 64, jnp.tanh(lo), lo)
            up = _dg(lo.astype(BF16), wl_ref[...], ((1,), (0,)))
            lw = -DECAY_SCALE * _sigmoid(pr_ref[_P_W0:_P_W0 + 1, :] + up[:, :D])
            a = _sigmoid(pr_ref[_P_A0:_P_A0 + 1, :] + up[:, D:])
            g = _cumsum_rows(tri, lw)
            ke = k * (1.0 + (a - 1.0) * pr_ref[_P_KA:_P_KA + 1, :])
            kkraw = k * pr_ref[_P_KK:_P_KK + 1, :]
            chunk_vals[j] = (rows, (r, ke, v, lw, g, a, kkraw))
            return chunk_vals[j]

        def make_task(j, pairs):
            def get_ops():
                _, arrs = chunk_level(j)
                out = []
                for p in pairs:
                    ln = slice(p * LANES, (p + 1) * LANES)
                    out.append(tuple(t[:, ln] for t in arrs) + (
                        pr_ref[_P_RK:_P_RK + 1, ln], pr_ref[_P_LNW:_P_LNW + 1, ln],
                        pr_ref[_P_LNB:_P_LNB + 1, ln]))
                return out

            def emit(p, y):
                rows, _ = chunk_level(j)
                ln = slice(p * LANES, (p + 1) * LANES)
                o_ref[rows, ln] = (y * _silu(gr_ref[rows, ln])).astype(o_ref.dtype)

            return _rwkv_task(get_ops, pairs, state, masks, emit)

        per_group = RW_PAIRS // RW_GROUPS
        groups = [list(range(gi * per_group, (gi + 1) * per_group)) for gi in range(RW_GROUPS)]
        _run_skewed([make_task(j, grp) for j in range(RW_UNROLL) for grp in groups], RW_SKEW)

        for p in range(RW_PAIRS):
            st_ref[p] = state[p]
        carry[0:1, 0:D] = last["r"]
        carry[0:1, D:2 * D] = last["k"]
        carry[0:1, 2 * D:3 * D] = last["v"]
        carry[0:1, 3 * D:3 * D + LANES] = last["l"]
        return 0

    lax.fori_loop(0, TS // (n * RW_UNROLL), body, 0)


def _rwkv(zr, zk, zv, zl, gr, pr, mul, wl):
    bsz, s, _ = zr.shape
    seq = lambda w: pl.BlockSpec((None, TS, w), lambda b, i: (b, i, 0))
    return pl.pallas_call(
        _rwkv_body,
        grid=(bsz, s // TS),
        in_specs=[seq(D), seq(D), seq(D), seq(LANES), seq(D),
                  _const_spec(pr.shape), _const_spec(mul.shape), _const_spec(wl.shape)],
        out_specs=seq(D),
        out_shape=jax.ShapeDtypeStruct((bsz, s, D), BF16),
        scratch_shapes=[pltpu.VMEM((8, 3 * D + LANES), F32),
                        pltpu.VMEM((RW_PAIRS, LANES, LANES), F32)],
        compiler_params=_params(("parallel", "arbitrary")),
        name="rwkv",
    )(zr, zk, zv, zl, gr, pr, mul, wl)


def _log_sigmoid(x):
    return jnp.minimum(x, 0.0) - jnp.log(1.0 + jnp.exp(-jnp.abs(x)))


def _mlstm_task(rows, refs, state, consts):
    (xm_ref, gm_ref, cw_ref, cb_ref, wq_ref, wk_ref, wv_ref, wif_ref, wift_ref, bif_ref,
     bift_ref, nrm_ref, skp_ref, o_ref) = refs
    tri, causal = consts
    n = CHUNK
    nn, nt, tn = ((1,), (0,)), ((1,), (1,)), ((0,), (0,))
    hs = range(ML_HEADS)
    lns = [slice(h * ML_HD, (h + 1) * ML_HD) for h in hs]
    k_scale = ML_HD ** -0.5

    xm = xm_ref[rows, :]
    xc = _silu(_conv4(xm, state["prev8"], cw_ref, cb_ref))
    state["prev8"] = xm[n - 8:, :]
    yield

    xcb = xc.astype(BF16)
    xmb = xm.astype(BF16)
    qb = [_dg(xcb[:, ln], wq_ref[h], nn).astype(BF16) for h, ln in enumerate(lns)]
    kf = [_dg(xcb[:, ln], wk_ref[h], nn) for h, ln in enumerate(lns)]
    vf = [_dg(xmb[:, ln], wv_ref[h], nn) for h, ln in enumerate(lns)]
    kb = [t.astype(BF16) for t in kf]
    vb = [t.astype(BF16) for t in vf]
    yield

    qkv = jnp.concatenate(qb + kb + vb, axis=1)
    gc = bif_ref[...] + _dg(qkv, wif_ref[...], nn)
    gt = bift_ref[...] + _dg(wift_ref[...], qkv, nt)
    bc_c = _mm_exact_lhs(tri, _log_sigmoid(gc))
    bc_t = _mm_exact_rhs(_log_sigmoid(gt), tri, nt)
    li_c = [gc[:, h:h + 1] for h in hs]
    li_t = [gt[h:h + 1, :] for h in hs]
    b_c = [bc_c[:, ML_HEADS + h:ML_HEADS + h + 1] for h in hs]
    b_t = [bc_t[ML_HEADS + h:ML_HEADS + h + 1, :] for h in hs]
    qk = [_dg(qb[h], kb[h], nt) for h in hs]
    log_d = [jnp.where(causal, b_c[h] - b_t[h] + li_t[h], -jnp.inf) for h in hs]
    b_last = [b_c[h][n - 1:n, :] for h in hs]
    log_g = [b_last[h] - b_c[h] + li_c[h] for h in hs]
    yield

    m_prev = [state["m", h] for h in hs]
    ct = [state["ct", h] for h in hs]
    nv = [state["nv", h] for h in hs]
    qc = [_dg(qb[h], ct[h].astype(BF16), nn) for h in hs]
    log_inter = [b_c[h] + m_prev[h] for h in hs]
    m_t = [jnp.maximum(jnp.max(log_d[h], axis=-1, keepdims=True), log_inter[h]) for h in hs]
    s = [qk[h] * k_scale * jnp.exp(log_d[h] - m_t[h]) for h in hs]
    w_inter = [jnp.exp(log_inter[h] - m_t[h]) for h in hs]
    m_new = [jnp.maximum(b_last[h] + m_prev[h], jnp.max(log_g[h], axis=0, keepdims=True))
             for h in hs]
    gcol = [jnp.exp(log_g[h] - m_new[h]) for h in hs]
    decay = [jnp.exp(b_last[h] + m_prev[h] - m_new[h]) for h in hs]
    yield

    sv = [_dg(s[h].astype(BF16), vb[h], nn) for h in hs]
    kgv = [_dg(kb[h], (gcol[h] * vf[h]).astype(BF16), tn) for h in hs]
    yield

    for h in hs:
        state["ct", h] = decay[h] * ct[h] + kgv[h] * k_scale
        state["nv", h] = decay[h] * nv[h] + k_scale * jnp.sum(gcol[h] * kf[h], axis=0,
                                                              keepdims=True)
        state["m", h] = m_new[h]
    yield

    for h in hs:
        ln = lns[h]
        num = sv[h] + w_inter[h] * qc[h]
        den = (jnp.sum(s[h], axis=-1, keepdims=True)
               + w_inter[h] * jnp.sum(qb[h].astype(F32) * nv[h], axis=-1, keepdims=True))
        hh = num / (jnp.maximum(jnp.abs(den), jnp.exp(-m_t[h])) + ML_EPS)
        mu = jnp.mean(hh, axis=-1, keepdims=True)
        dlt = hh - mu
        var = jnp.mean(dlt * dlt, axis=-1, keepdims=True)
        hn = dlt * lax.rsqrt(var + ML_LN_EPS) * nrm_ref[:, ln]
        ym = hn + skp_ref[:, ln] * xc[:, ln]
        o_ref[rows, ln] = (ym * _silu(gm_ref[rows, ln])).astype(o_ref.dtype)


def _mlstm_body(xm_ref, gm_ref, cw_ref, cb_ref, wq_ref, wk_ref, wv_ref, wif_ref, wift_ref,
                bif_ref, bift_ref, nrm_ref, skp_ref, o_ref, prev8, ct_ref, n_ref, m_ref):
    n = CHUNK

    @pl.when(pl.program_id(1) == 0)
    def _():
        prev8[...] = jnp.zeros_like(prev8)
        ct_ref[...] = jnp.zeros_like(ct_ref)
        n_ref[...] = jnp.zeros_like(n_ref)
        m_ref[...] = jnp.zeros_like(m_ref)

    consts = (_tri_incl(n), _iota((n, n), 1) <= _iota((n, n), 0))
    refs = (xm_ref, gm_ref, cw_ref, cb_ref, wq_ref, wk_ref, wv_ref, wif_ref, wift_ref, bif_ref,
            bift_ref, nrm_ref, skp_ref, o_ref)

    def body(it, _):
        state = {"prev8": prev8[...]}
        for h in range(ML_HEADS):
            state["ct", h] = ct_ref[h]
            state["nv", h] = n_ref[0:1, h * ML_HD:(h + 1) * ML_HD]
            state["m", h] = m_ref[h:h + 1, 0:1]
        tasks = [_mlstm_task(pl.ds(pl.multiple_of((it * ML_UNROLL + j) * n, n), n), refs, state,
                             consts) for j in range(ML_UNROLL)]
        _run_skewed(tasks, ML_SKEW)
        prev8[...] = state["prev8"]
        for h in range(ML_HEADS):
            ct_ref[h] = state["ct", h]
            n_ref[0:1, h * ML_HD:(h + 1) * ML_HD] = state["nv", h]
            m_ref[h:h + 1, :] = jnp.broadcast_to(state["m", h], (1, LANES))
        return 0

    lax.fori_loop(0, TS // (n * ML_UNROLL), body, 0)


def _mlstm(xm, gm, cw, cb, wq, wk, wv, wif, wift, bif, bift, nrm, skp):
    bsz, s, _ = xm.shape
    seq = pl.BlockSpec((None, TS, D), lambda b, i: (b, i, 0))
    consts = (cw, cb, wq, wk, wv, wif, wift, bif, bift, nrm, skp)
    return pl.pallas_call(
        _mlstm_body,
        grid=(bsz, s // TS),
        in_specs=[seq, seq] + [_const_spec(a.shape) for a in consts],
        out_specs=seq,
        out_shape=jax.ShapeDtypeStruct((bsz, s, D), BF16),
        scratch_shapes=[pltpu.VMEM((8, D), F32),
                        pltpu.VMEM((ML_HEADS, ML_HD, ML_HD), F32),
                        pltpu.VMEM((8, D), F32),
                        pltpu.VMEM((8, LANES), F32)],
        compiler_params=_params(("parallel", "arbitrary")),
        name="mlstm",
    )(xm, gm, *consts)


def _mix_out(h_ref, ya_ref, yb_ref, wo_ref, p_ref, pn_ref, pu_ref, pg_ref):
    h = h_ref[...] + _dg(ya_ref[...], wo_ref[:D, :], ((1,), (0,)))
    h = h + _dg(yb_ref[...], wo_ref[D:, :], ((1,), (0,)))
    hn = _rms(h, pn_ref[...]).astype(BF16)
    gate = _sigmoid(_dg(hn, pg_ref[...], ((1,), (0,))))
    return h + _dg(p_ref[...].astype(BF16), pu_ref[...], ((1,), (0,))) * gate


def _mid_body(h_ref, ya_ref, yb_ref, wo_ref, p_ref, pn_ref, pu_ref, pg_ref, mn_ref, wi_ref,
              ho_ref, xl_ref, g_ref):
    h = _mix_out(h_ref, ya_ref, yb_ref, wo_ref, p_ref, pn_ref, pu_ref, pg_ref)
    ho_ref[...] = h
    xn = _rms(h, mn_ref[...]).astype(BF16)
    xl_ref[...] = _dg(xn, wi_ref[:, :LRU_W], ((1,), (0,)))
    g_ref[...] = _dg(xn, wi_ref[:, LRU_W:], ((1,), (0,)))


def _final_body(h_ref, ya_ref, yb_ref, wo_ref, p_ref, pn_ref, pu_ref, pg_ref, fn_ref, o_ref):
    h = _mix_out(h_ref, ya_ref, yb_ref, wo_ref, p_ref, pn_ref, pu_ref, pg_ref)
    o_ref[...] = _rms(h, fn_ref[...])


def _row_spec(w, half=None):
    if half is None:
        return pl.BlockSpec((TM, w), lambda i: (i, 0))
    return pl.BlockSpec((TM, w), lambda i: (i, half))


def _mid(h, ya, yb, wo, p, pn, pu, pg, mn, wi):
    t = h.shape[0]
    return pl.pallas_call(
        _mid_body,
        grid=(t // TM,),
        in_specs=[_row_spec(D), _row_spec(D), _row_spec(D), _const_spec(wo.shape),
                  _row_spec(PLE), _const_spec(pn.shape), _const_spec(pu.shape),
                  _const_spec(pg.shape), _const_spec(mn.shape), _const_spec(wi.shape)],
        out_specs=[_row_spec(D), _row_spec(LRU_W), _row_spec(LRU_W)],
        out_shape=[jax.ShapeDtypeStruct((t, D), F32), jax.ShapeDtypeStruct((t, LRU_W), F32),
                   jax.ShapeDtypeStruct((t, LRU_W), F32)],
        compiler_params=_params(("parallel",)),
        name="mid",
    )(h, ya, yb, wo, p, pn, pu, pg, mn, wi)


def _final(h, y, wo, p, pn, pu, pg, fn):
    t = h.shape[0]
    return pl.pallas_call(
        _final_body,
        grid=(t // TM,),
        in_specs=[_row_spec(D), _row_spec(D, 0), _row_spec(D, 1), _const_spec(wo.shape),
                  _row_spec(PLE), _const_spec(pn.shape), _const_spec(pu.shape),
                  _const_spec(pg.shape), _const_spec(fn.shape)],
        out_specs=_row_spec(D),
        out_shape=jax.ShapeDtypeStruct((t, D), F32),
        compiler_params=_params(("parallel",)),
        name="final",
    )(h, y, y, wo, p, pn, pu, pg, fn)


def _rglru_body(xl_ref, g_ref, cw_ref, cb_ref, wr_ref, br_ref, wi_ref, bi_ref, lam_ref, o_ref,
                prev8, hc):
    n = CHUNK
    w = LRU_STRIP

    @pl.when(pl.program_id(2) == 0)
    def _():
        prev8[...] = jnp.zeros_like(prev8)
        hc[...] = jnp.zeros_like(hc)

    row8 = _iota((n // 8, 8, w), 1)

    def chunk_body(c, _):
        rows = pl.ds(pl.multiple_of(c * n, n), n)
        xl = xl_ref[rows, :]
        xc = _conv4(xl, prev8[...], cw_ref, cb_ref)
        prev8[...] = xl[n - 8:, :]
        rg = []
        ig = []
        for j in range(w // LRU_BLK):
            xb = xc[:, j * LRU_BLK:(j + 1) * LRU_BLK].astype(BF16)
            rg.append(_dg(xb, wr_ref[j], ((1,), (0,))))
            ig.append(_dg(xb, wi_ref[j], ((1,), (0,))))
        rg = _sigmoid(jnp.concatenate(rg, axis=1) + br_ref[...])
        ig = _sigmoid(jnp.concatenate(ig, axis=1) + bi_ref[...])
        log_a = -LRU_C * rg * _softplus(-lam_ref[...])
        a = jnp.exp(log_a)
        mult = jnp.sqrt(jnp.maximum(1.0 - a * a, 0.0))
        u = xc * ig * mult
        u = u.reshape(n // 8, 8, w)
        a = a.reshape(n // 8, 8, w)
        for d in (1, 2, 4):
            keep = row8 >= d
            u = u + a * jnp.where(keep, pltpu.roll(u, d, 1), 0.0)
            a = a * jnp.where(keep, pltpu.roll(a, d, 1), 1.0)
        carry = hc[0:1, :]
        hs = []
        for j in range(n // 8):
            hj = u[j] + a[j] * carry
            carry = hj[7:8, :]
            hs.append(hj)
        hc[0:1, :] = carry
        h = jnp.concatenate(hs, axis=0)
        o_ref[rows, :] = (h * _silu(g_ref[rows, :])).astype(o_ref.dtype)
        return 0

    lax.fori_loop(0, LRU_TS // n, chunk_body, 0, unroll=LRU_UNROLL)


def _rglru(xl, g, cw, cb, wr, br, wi, bi, lam):
    bsz, s, _ = xl.shape
    ns = LRU_W // LRU_STRIP
    seq = pl.BlockSpec((None, LRU_TS, LRU_STRIP), lambda b, j, i: (b, i, j))
    vec = lambda r: pl.BlockSpec((r, LRU_STRIP), lambda b, j, i: (0, j))
    blk = pl.BlockSpec((LRU_STRIP // LRU_BLK, LRU_BLK, LRU_BLK), lambda b, j, i: (j, 0, 0))
    return pl.pallas_call(
        _rglru_body,
        grid=(bsz, ns, s // LRU_TS),
        in_specs=[seq, seq, vec(CONV_W), vec(1), blk, vec(1), blk, vec(1), vec(1)],
        out_specs=seq,
        out_shape=jax.ShapeDtypeStruct((bsz, s, LRU_W), BF16),
        scratch_shapes=[pltpu.VMEM((8, LRU_STRIP), F32), pltpu.VMEM((8, LRU_STRIP), F32)],
        compiler_params=_params(("parallel", "parallel", "arbitrary")),
        name="rglru",
    )(xl, g, cw, cb, wr, br, wi, bi, lam)


def _block4_tiles(w):
    nblk = w.shape[0]
    per = ML_HD // QKV_BLOCK
    eye = jnp.eye(per, dtype=w.dtype)
    wt = w.reshape(nblk // per, per, QKV_BLOCK, QKV_BLOCK)
    dense = jnp.einsum("tgio,gh->tgiho", wt, eye)
    return dense.reshape(nblk // per, ML_HD, ML_HD)


def kernel(x, p, mix_norm, pe_norm, final_norm, pe_up, pe_gate, ab_w_in, rwkv_mu, rwkv_mu_lora, rwkv_w0, rwkv_w_up, rwkv_a0, rwkv_a_up, rwkv_k_k, rwkv_k_a, rwkv_r_k, rwkv_ln_w, rwkv_ln_b, mlstm_conv_w, mlstm_conv_b, mlstm_wq, mlstm_wk, mlstm_wv, mlstm_w_if, mlstm_b_if, mlstm_norm, mlstm_skip, ab_w_out, c_w_in, c_conv_w, c_conv_b, c_wr, c_br, c_wi, c_bi, c_lambda, c_w_out):
    bsz, s, _ = x.shape
    t = bsz * s
    row = lambda a: a.reshape(1, -1)
    seq = lambda a: a.reshape(bsz, s, a.shape[-1])

    zr, zk, zv, xm, gr, gm, zl = _proj0(x.reshape(t, D), row(mix_norm[0]),
                                        ab_w_in[0].astype(BF16))

    pr = jnp.concatenate([rwkv_mu[0], row(rwkv_w0[0]), row(rwkv_a0[0]), row(rwkv_k_k[0]),
                          row(rwkv_k_a[0]), row(rwkv_r_k[0]), row(rwkv_ln_w[0]),
                          row(rwkv_ln_b[0]), jnp.zeros((6, D), F32)], axis=0)
    mul = rwkv_mu_lora[0].reshape(1, 128)
    zeros = jnp.zeros((64, D), F32)
    wl = jnp.concatenate([jnp.concatenate([rwkv_w_up[0], zeros], axis=1),
                          jnp.concatenate([zeros, rwkv_a_up[0]], axis=1)], axis=0)
    y_rwkv = _rwkv(seq(zr), seq(zk), seq(zv), seq(zl), seq(gr), pr, mul, wl.astype(BF16))

    wif = mlstm_w_if[0]
    y_m = _mlstm(seq(xm), seq(gm), mlstm_conv_w[0], row(mlstm_conv_b[0]),
                 _block4_tiles(mlstm_wq[0]).astype(BF16), _block4_tiles(mlstm_wk[0]).astype(BF16),
                 _block4_tiles(mlstm_wv[0]).astype(BF16), wif.astype(BF16),
                 wif.T.astype(BF16),
                 row(mlstm_b_if[0]), mlstm_b_if[0].reshape(-1, 1),
                 row(mlstm_norm[0]), row(mlstm_skip[0]))

    h1, xl, g1 = _mid(x.reshape(t, D), y_rwkv.reshape(t, D), y_m.reshape(t, D),
                      ab_w_out[0].astype(BF16), p[0].reshape(t, PLE), row(pe_norm[0]),
                      pe_up[0].astype(BF16), pe_gate[0].astype(BF16), row(mix_norm[1]),
                      c_w_in[0].astype(BF16))

    y_lru = _rglru(seq(xl), seq(g1), c_conv_w[0], row(c_conv_b[0]), c_wr[0].astype(BF16),
                   row(c_br[0]), c_wi[0].astype(BF16), row(c_bi[0]), row(c_lambda[0]))
    out = _final(h1, y_lru.reshape(t, LRU_W), c_w_out[0].astype(BF16), p[1].reshape(t, PLE),
                 row(pe_norm[1]), pe_up[1].astype(BF16), pe_gate[1].astype(BF16),
                 row(final_norm))
    return out.reshape(bsz, s, D)
```

```python
import jax
import jax.numpy as jnp
from jax import lax
from jax.experimental import pallas as pl
from jax.experimental.pallas import tpu as pltpu

F32 = jnp.float32
BF16 = jnp.bfloat16

D = 1024
LANES = 128
CHUNK = 64
NORM_EPS = 1e-6
PLE = 256
RW_HEADS = 16
RW_HD = 64
RW_PAIRS = RW_HEADS // 2
DECAY_SCALE = 0.6065306597126334
RW_GN_EPS = 64e-5
ML_HEADS = 4
ML_HD = 256
ML_EPS = 1e-6
ML_LN_EPS = 1e-5
QKV_BLOCK = 4
CONV_W = 4
LRU_W = 2 * D
LRU_BLK = 128
LRU_C = 8.0
SQRT_FLOOR = 1e-30
AB_IN = 4 * D + 128 + 2 * D

TM = 512
TS = 512
LRU_TS = 1024
LRU_STRIP = 1024
RW_UNROLL = 4
RW_GROUPS = 1
RW_SKEW = 2
ML_UNROLL = 8
ML_SKEW = 2
LRU_UNROLL = 4
VMEM_LIMIT = 56 * 1024 * 1024

assert CHUNK == RW_HD and ML_SKEW >= 2 and RW_GROUPS * RW_SKEW >= 1


def _dg(a, b, dims):
    return lax.dot_general(a, b, (dims, ((), ())), preferred_element_type=F32)


def _split2(x):
    hi = x.astype(BF16)
    lo = (x - hi.astype(F32)).astype(BF16)
    return hi, lo


def _split3(x):
    hi = x.astype(BF16)
    r = x - hi.astype(F32)
    mid = r.astype(BF16)
    lo = (r - mid.astype(F32)).astype(BF16)
    return hi, mid, lo


def _cumsum_rows(tri_bf16, x):
    hi, lo = _split2(x)
    return _dg(tri_bf16, hi, ((1,), (0,))) + _dg(tri_bf16, lo, ((1,), (0,)))


def _mm_exact_lhs(a_bf16, b, dims=((1,), (0,))):
    h, m, l = _split3(b)
    return _dg(a_bf16, h, dims) + (_dg(a_bf16, m, dims) + _dg(a_bf16, l, dims))


def _mm_exact_rhs(a, b_bf16, dims=((1,), (0,))):
    h, m, l = _split3(a)
    return _dg(h, b_bf16, dims) + (_dg(m, b_bf16, dims) + _dg(l, b_bf16, dims))


def _rms(x, g):
    return x * lax.rsqrt(jnp.mean(x * x, axis=-1, keepdims=True) + NORM_EPS) * g


def _sigmoid(x):
    return jax.nn.sigmoid(x)


def _silu(x):
    return x * _sigmoid(x)


def _softplus(x):
    return jnp.maximum(x, 0.0) + jnp.log(1.0 + jnp.exp(-jnp.abs(x)))


def _iota(shape, axis):
    return lax.broadcasted_iota(jnp.int32, shape, axis)


def _tri_incl(n):
    return jnp.where(_iota((n, n), 1) <= _iota((n, n), 0), 1.0, 0.0).astype(BF16)


def _shift_rows(x, d, prev8):
    xs = pltpu.roll(x, d, 0)
    ps = pltpu.roll(prev8, d, 0)
    head = jnp.where(_iota((8, x.shape[1]), 0) < d, ps, xs[:8])
    return jnp.concatenate([head, xs[8:]], axis=0)


def _conv4(x, prev8, w_ref, b_ref):
    acc = x * w_ref[3:4, :] + b_ref[...]
    for d in (1, 2, 3):
        acc = acc + _shift_rows(x, d, prev8) * w_ref[3 - d:4 - d, :]
    return acc


def _run_skewed(tasks, skew):
    pending = list(tasks)
    active = []
    tick = 0
    while pending or active:
        if pending and tick % skew == 0:
            active.append(pending.pop(0))
        for t in list(active):
            try:
                next(t)
            except StopIteration:
                active.remove(t)
        tick += 1


def _const_spec(shape):
    nd = len(shape)
    return pl.BlockSpec(shape, lambda *_: (0,) * nd, pipeline_mode=pl.Buffered(1))


def _params(sem):
    return pltpu.CompilerParams(dimension_semantics=sem, vmem_limit_bytes=VMEM_LIMIT)


def _proj0_body(x_ref, g_ref, w_ref, zr, zk, zv, xm, gr, gm, zl):
    xn = _rms(x_ref[...], g_ref[...]).astype(BF16)
    for j, o in enumerate((zr, zk, zv)):
        o[...] = _dg(xn, w_ref[:, j * D:(j + 1) * D], ((1,), (0,)))
    zl[...] = _dg(xn, w_ref[:, 3 * D:3 * D + 128], ((1,), (0,)))
    for j, o in enumerate((xm, gr, gm)):
        o[...] = _dg(xn, w_ref[:, 3 * D + 128 + j * D:3 * D + 128 + (j + 1) * D], ((1,), (0,)))


def _proj0(x2, g, w):
    t = x2.shape[0]
    wide = jax.ShapeDtypeStruct((t, D), F32)
    row = pl.BlockSpec((TM, D), lambda i: (i, 0))
    return pl.pallas_call(
        _proj0_body,
        grid=(t // TM,),
        in_specs=[row, _const_spec((1, D)), _const_spec((D, AB_IN))],
        out_specs=[row] * 6 + [pl.BlockSpec((TM, 128), lambda i: (i, 0))],
        out_shape=[wide] * 6 + [jax.ShapeDtypeStruct((t, 128), F32)],
        compiler_params=_params(("parallel",)),
        name="proj0",
    )(x2, g, w)


_P_MUR, _P_MUK, _P_MUV, _P_W0, _P_A0, _P_KK, _P_KA, _P_RK, _P_LNW, _P_LNB = range(10)


def _rwkv_task(get_ops, pairs, state, masks, emit):
    m_top, rowhead, strict, incl, eye = masks
    n = CHUNK
    rng = range(len(pairs))
    bf = lambda t: t.astype(BF16)
    nn, nt, tn = ((1,), (0,)), ((1,), (1,)), ((0,), (0,))

    def stack(x):
        z = jnp.zeros_like(x)
        return jnp.concatenate([jnp.where(m_top, x, z), jnp.where(m_top, z, x)], axis=0)

    def unstack_col(c):
        return jnp.where(m_top, c[:n], c[n:])

    def head_sum(x):
        return unstack_col(jnp.sum(stack(x), axis=-1, keepdims=True))

    pre = []
    for r, ke, v, lw, g, a, kkraw, rk, lnw, lnb in get_ops():
        kk = kkraw / jnp.maximum(jnp.sqrt(head_sum(jnp.square(kkraw))), 1e-12)
        bonus = head_sum(r * ke * rk)
        g_last = g[n - 1:n, :]
        b = kk * a
        e_neg = jnp.exp(-g)
        e_tail = jnp.exp(g_last - g)
        ag = bf(-kk * jnp.exp(g - lw))
        vb = bf(v)
        pre.append(dict(
            bonus=bonus, v=v, lnw=lnw, lnb=lnb, vb=vb, ag=ag, rg=bf(r * jnp.exp(g)),
            bk_s=jnp.concatenate([stack(bf(b * e_neg)), stack(bf(ke * e_neg))], axis=0),
            bkd=jnp.concatenate([bf(b * e_tail), bf(ke * e_tail)], axis=0),
            e_last_col=jnp.transpose(jnp.broadcast_to(jnp.exp(g_last), (LANES, LANES)))))
    yield

    sc = [_dg(jnp.concatenate([q["ag"], q["rg"]], axis=0), q["bk_s"], nt) for q in pre]
    n_ab_f = [jnp.where(strict, t[:n, :LANES], 0.0) for t in sc]
    n_ab = [bf(t) for t in n_ab_f]
    nr_k = [bf(jnp.concatenate([jnp.where(strict, t[:n, LANES:], 0.0),
                                jnp.where(incl, t[n:, LANES:], 0.0)], axis=0)) for t in sc]
    a_rb = [bf(jnp.where(incl, t[n:, :LANES], 0.0)) for t in sc]
    yield

    kv = [_dg(nr_k[i], stack(pre[i]["vb"]), nn) for i in rng]
    yield

    pk = n_ab
    t = [eye + n_ab_f[i] for i in rng]
    p_next = [bf(_dg(pk[i], stack(pk[i]), nn)) for i in rng]
    yield
    for step in range(1, 6):
        pk = p_next
        if step < 5:
            m = [_dg(pk[i], jnp.concatenate([stack(bf(t[i])), stack(pk[i])], axis=1), nn)
                 for i in rng]
            p_next = [bf(m[i][:, LANES:]) for i in rng]
            t = [t[i] + m[i][:, :LANES] for i in rng]
        else:
            t = [t[i] + _dg(pk[i], stack(bf(t[i])), nn) for i in rng]
        yield

    x = [_dg(bf(t[i]), jnp.concatenate([stack(pre[i]["ag"]), stack(bf(kv[i][:n]))], axis=1), nn)
         for i in rng]
    yield

    sts = [state[p] for p in pairs]
    stb = [bf(s_) for s_ in sts]
    u = [_dg(bf(x[i][:, :LANES]), stb[i], nn) + x[i][:, LANES:] for i in rng]
    ub = [bf(u_) for u_ in u]
    yield

    for i, p in enumerate(pairs):
        upd = _dg(pre[i]["bkd"], jnp.concatenate([ub[i], pre[i]["vb"]], axis=0), tn)
        state[p] = pre[i]["e_last_col"] * sts[i] + jnp.where(rowhead, upd, 0.0)
    y = [_dg(jnp.concatenate([pre[i]["rg"], a_rb[i]], axis=1),
             jnp.concatenate([stb[i], stack(ub[i])], axis=0), nn) + kv[i][n:] for i in rng]
    yield

    for i, p in enumerate(pairs):
        q = pre[i]
        dlt = y[i] - head_sum(y[i]) * (1.0 / RW_HD)
        var = head_sum(dlt * dlt) * (1.0 / RW_HD)
        yn = dlt * lax.rsqrt(var + RW_GN_EPS)
        emit(p, yn * q["lnw"] + q["lnb"] + q["bonus"] * q["v"])


def _rwkv_body(zr_ref, zk_ref, zv_ref, zl_ref, gr_ref, pr_ref, mul_ref, wl_ref, o_ref,
               carry, st_ref):
    n = CHUNK

    @pl.when(pl.program_id(1) == 0)
    def _():
        carry[...] = jnp.zeros_like(carry)
        st_ref[...] = jnp.zeros_like(st_ref)

    lane = _iota((1, LANES), 1)
    m_top = lane < RW_HD
    r2 = _iota((2 * n, LANES), 0)
    c2 = _iota((2 * n, LANES), 1)
    rowhead = (r2 < n) == (c2 < RW_HD)
    rw = _iota((n, LANES), 0)
    sw = jnp.bitwise_and(_iota((n, LANES), 1), RW_HD - 1)
    strict = sw < rw
    incl = sw <= rw
    eye = jnp.where(sw == rw, 1.0, 0.0)
    masks = (m_top, rowhead, strict, incl, eye)
    tri = _tri_incl(n)
    first_row = _iota((n, 1), 0) == 0

    def shift_mix(z, prev_row, mu):
        prev = jnp.where(first_row, prev_row, pltpu.roll(z, 1, 0))
        return z + (prev - z) * mu

    def body(it, _):
        state = {p: st_ref[p] for p in range(RW_PAIRS)}
        last = dict(r=carry[0:1, 0:D], k=carry[0:1, D:2 * D], v=carry[0:1, 2 * D:3 * D],
                    l=carry[0:1, 3 * D:3 * D + LANES])
        chunk_vals = {}

        def chunk_level(j):
            if j in chunk_vals:
                return chunk_vals[j]
            rows = pl.ds(pl.multiple_of((it * RW_UNROLL + j) * n, n), n)
            zr = zr_ref[rows, :]
            zk = zk_ref[rows, :]
            zv = zv_ref[rows, :]
            zl = zl_ref[rows, :]
            r = shift_mix(zr, last["r"], pr_ref[_P_MUR:_P_MUR + 1, :])
            k = shift_mix(zk, last["k"], pr_ref[_P_MUK:_P_MUK + 1, :])
            v = shift_mix(zv, last["v"], pr_ref[_P_MUV:_P_MUV + 1, :])
            lo = shift_mix(zl, last["l"], mul_ref[...])
            last.update(r=zr[n - 1:n, :], k=zk[n - 1:n, :], v=zv[n - 1:n, :], l=zl[n - 1:n, :])
            lo = jnp.where(_iota((n, LANES), 1) < 64, jnp.tanh(lo), lo)
            up = _dg(lo.astype(BF16), wl_ref[...], ((1,), (0,)))
            lw = -DECAY_SCALE * _sigmoid(pr_ref[_P_W0:_P_W0 + 1, :] + up[:, :D])
            a = _sigmoid(pr_ref[_P_A0:_P_A0 + 1, :] + up[:, D:])
            g = _cumsum_rows(tri, lw)
            ke = k * (1.0 + (a - 1.0) * pr_ref[_P_KA:_P_KA + 1, :])
            kkraw = k * pr_ref[_P_KK:_P_KK + 1, :]
            chunk_vals[j] = (rows, (r, ke, v, lw, g, a, kkraw))
            return chunk_vals[j]

        def make_task(j, pairs):
            def get_ops():
                _, arrs = chunk_level(j)
                out = []
                for p in pairs:
                    ln = slice(p * LANES, (p + 1) * LANES)
                    out.append(tuple(t[:, ln] for t in arrs) + (
                        pr_ref[_P_RK:_P_RK + 1, ln], pr_ref[_P_LNW:_P_LNW + 1, ln],
                        pr_ref[_P_LNB:_P_LNB + 1, ln]))
                return out

            def emit(p, y):
                rows, _ = chunk_level(j)
                ln = slice(p * LANES, (p + 1) * LANES)
                o_ref[rows, ln] = (y * _silu(gr_ref[rows, ln])).astype(o_ref.dtype)

            return _rwkv_task(get_ops, pairs, state, masks, emit)

        per_group = RW_PAIRS // RW_GROUPS
        groups = [list(range(gi * per_group, (gi + 1) * per_group)) for gi in range(RW_GROUPS)]
        _run_skewed([make_task(j, grp) for j in range(RW_UNROLL) for grp in groups], RW_SKEW)

        for p in range(RW_PAIRS):
            st_ref[p] = state[p]
        carry[0:1, 0:D] = last["r"]
        carry[0:1, D:2 * D] = last["k"]
        carry[0:1, 2 * D:3 * D] = last["v"]
        carry[0:1, 3 * D:3 * D + LANES] = last["l"]
        return 0

    lax.fori_loop(0, TS // (n * RW_UNROLL), body, 0)


def _rwkv(zr, zk, zv, zl, gr, pr, mul, wl):
    bsz, s, _ = zr.shape
    seq = lambda w: pl.BlockSpec((None, TS, w), lambda b, i: (b, i, 0))
    return pl.pallas_call(
        _rwkv_body,
        grid=(bsz, s // TS),
        in_specs=[seq(D), seq(D), seq(D), seq(LANES), seq(D),
                  _const_spec(pr.shape), _const_spec(mul.shape), _const_spec(wl.shape)],
        out_specs=seq(D),
        out_shape=jax.ShapeDtypeStruct((bsz, s, D), BF16),
        scratch_shapes=[pltpu.VMEM((8, 3 * D + LANES), F32),
                        pltpu.VMEM((RW_PAIRS, LANES, LANES), F32)],
        compiler_params=_params(("parallel", "arbitrary")),
        name="rwkv",
    )(zr, zk, zv, zl, gr, pr, mul, wl)


def _log_sigmoid(x):
    return jnp.minimum(x, 0.0) - jnp.log(1.0 + jnp.exp(-jnp.abs(x)))


def _mlstm_task(rows, refs, state, consts):
    (xm_ref, gm_ref, cw_ref, cb_ref, wq_ref, wk_ref, wv_ref, wif_ref, wift_ref, bif_ref,
     bift_ref, nrm_ref, skp_ref, o_ref) = refs
    tri, causal = consts
    n = CHUNK
    nn, nt, tn = ((1,), (0,)), ((1,), (1,)), ((0,), (0,))
    hs = range(ML_HEADS)
    lns = [slice(h * ML_HD, (h + 1) * ML_HD) for h in hs]
    k_scale = ML_HD ** -0.5

    xm = xm_ref[rows, :]
    xc = _silu(_conv4(xm, state["prev8"], cw_ref, cb_ref))
    state["prev8"] = xm[n - 8:, :]
    yield

    xcb = xc.astype(BF16)
    xmb = xm.astype(BF16)
    qb = [_dg(xcb[:, ln], wq_ref[h], nn).astype(BF16) for h, ln in enumerate(lns)]
    kf = [_dg(xcb[:, ln], wk_ref[h], nn) for h, ln in enumerate(lns)]
    vf = [_dg(xmb[:, ln], wv_ref[h], nn) for h, ln in enumerate(lns)]
    kb = [t.astype(BF16) for t in kf]
    vb = [t.astype(BF16) for t in vf]
    yield

    qkv = jnp.concatenate(qb + kb + vb, axis=1)
    gc = bif_ref[...] + _dg(qkv, wif_ref[...], nn)
    gt = bift_ref[...] + _dg(wift_ref[...], qkv, nt)
    bc_c = _mm_exact_lhs(tri, _log_sigmoid(gc))
    bc_t = _mm_exact_rhs(_log_sigmoid(gt), tri, nt)
    li_c = [gc[:, h:h + 1] for h in hs]
    li_t = [gt[h:h + 1, :] for h in hs]
    b_c = [bc_c[:, ML_HEADS + h:ML_HEADS + h + 1] for h in hs]
    b_t = [bc_t[ML_HEADS + h:ML_HEADS + h + 1, :] for h in hs]
    qk = [_dg(qb[h], kb[h], nt) for h in hs]
    log_d = [jnp.where(causal, b_c[h] - b_t[h] + li_t[h], -jnp.inf) for h in hs]
    b_last = [b_c[h][n - 1:n, :] for h in hs]
    log_g = [b_last[h] - b_c[h] + li_c[h] for h in hs]
    yield

    m_prev = [state["m", h] for h in hs]
    ct = [state["ct", h] for h in hs]
    nv = [state["nv", h] for h in hs]
    qc = [_dg(qb[h], ct[h].astype(BF16), nn) for h in hs]
    log_inter = [b_c[h] + m_prev[h] for h in hs]
    m_t = [jnp.maximum(jnp.max(log_d[h], axis=-1, keepdims=True), log_inter[h]) for h in hs]
    s = [qk[h] * k_scale * jnp.exp(log_d[h] - m_t[h]) for h in hs]
    w_inter = [jnp.exp(log_inter[h] - m_t[h]) for h in hs]
    m_new = [jnp.maximum(b_last[h] + m_prev[h], jnp.max(log_g[h], axis=0, keepdims=True))
             for h in hs]
    gcol = [jnp.exp(log_g[h] - m_new[h]) for h in hs]
    decay = [jnp.exp(b_last[h] + m_prev[h] - m_new[h]) for h in hs]
    yield

    sv = [_dg(s[h].astype(BF16), vb[h], nn) for h in hs]
    kgv = [_dg(kb[h], (gcol[h] * vf[h]).astype(BF16), tn) for h in hs]
    yield

    for h in hs:
        state["ct", h] = decay[h] * ct[h] + kgv[h] * k_scale
        state["nv", h] = decay[h] * nv[h] + k_scale * jnp.sum(gcol[h] * kf[h], axis=0,
                                                              keepdims=True)
        state["m", h] = m_new[h]
    yield

    for h in hs:
        ln = lns[h]
        num = sv[h] + w_inter[h] * qc[h]
        den = (jnp.sum(s[h], axis=-1, keepdims=True)
               + w_inter[h] * jnp.sum(qb[h].astype(F32) * nv[h], axis=-1, keepdims=True))
        hh = num / (jnp.maximum(jnp.abs(den), jnp.exp(-m_t[h])) + ML_EPS)
        mu = jnp.mean(hh, axis=-1, keepdims=True)
        dlt = hh - mu
        var = jnp.mean(dlt * dlt, axis=-1, keepdims=True)
        hn = dlt * lax.rsqrt(var + ML_LN_EPS) * nrm_ref[:, ln]
        ym = hn + skp_ref[:, ln] * xc[:, ln]
        o_ref[rows, ln] = (ym * _silu(gm_ref[rows, ln])).astype(o_ref.dtype)


def _mlstm_body(xm_ref, gm_ref, cw_ref, cb_ref, wq_ref, wk_ref, wv_ref, wif_ref, wift_ref,
                bif_ref, bift_ref, nrm_ref, skp_ref, o_ref, prev8, ct_ref, n_ref, m_ref):
    n = CHUNK

    @pl.when(pl.program_id(1) == 0)
    def _():
        prev8[...] = jnp.zeros_like(prev8)
        ct_ref[...] = jnp.zeros_like(ct_ref)
        n_ref[...] = jnp.zeros_like(n_ref)
        m_ref[...] = jnp.zeros_like(m_ref)

    consts = (_tri_incl(n), _iota((n, n), 1) <= _iota((n, n), 0))
    refs = (xm_ref, gm_ref, cw_ref, cb_ref, wq_ref, wk_ref, wv_ref, wif_ref, wift_ref, bif_ref,
            bift_ref, nrm_ref, skp_ref, o_ref)

    def body(it, _):
        state = {"prev8": prev8[...]}
        for h in range(ML_HEADS):
            state["ct", h] = ct_ref[h]
            state["nv", h] = n_ref[0:1, h * ML_HD:(h + 1) * ML_HD]
            state["m", h] = m_ref[h:h + 1, 0:1]
        tasks = [_mlstm_task(pl.ds(pl.multiple_of((it * ML_UNROLL + j) * n, n), n), refs, state,
                             consts) for j in range(ML_UNROLL)]
        _run_skewed(tasks, ML_SKEW)
        prev8[...] = state["prev8"]
        for h in range(ML_HEADS):
            ct_ref[h] = state["ct", h]
            n_ref[0:1, h * ML_HD:(h + 1) * ML_HD] = state["nv", h]
            m_ref[h:h + 1, :] = jnp.broadcast_to(state["m", h], (1, LANES))
        return 0

    lax.fori_loop(0, TS // (n * ML_UNROLL), body, 0)


def _mlstm(xm, gm, cw, cb, wq, wk, wv, wif, wift, bif, bift, nrm, skp):
    bsz, s, _ = xm.shape
    seq = pl.BlockSpec((None, TS, D), lambda b, i: (b, i, 0))
    consts = (cw, cb, wq, wk, wv, wif, wift, bif, bift, nrm, skp)
    return pl.pallas_call(
        _mlstm_body,
        grid=(bsz, s // TS),
        in_specs=[seq, seq] + [_const_spec(a.shape) for a in consts],
        out_specs=seq,
        out_shape=jax.ShapeDtypeStruct((bsz, s, D), BF16),
        scratch_shapes=[pltpu.VMEM((8, D), F32),
                        pltpu.VMEM((ML_HEADS, ML_HD, ML_HD), F32),
                        pltpu.VMEM((8, D), F32),
                        pltpu.VMEM((8, LANES), F32)],
        compiler_params=_params(("parallel", "arbitrary")),
        name="mlstm",
    )(xm, gm, *consts)


def _mix_out(h_ref, ya_ref, yb_ref, wo_ref, p_ref, pn_ref, pu_ref, pg_ref):
    h = h_ref[...] + _dg(ya_ref[...], wo_ref[:D, :], ((1,), (0,)))
    h = h + _dg(yb_ref[...], wo_ref[D:, :], ((1,), (0,)))
    hn = _rms(h, pn_ref[...]).astype(BF16)
    gate = _sigmoid(_dg(hn, pg_ref[...], ((1,), (0,))))
    return h + _dg(p_ref[...].astype(BF16), pu_ref[...], ((1,), (0,))) * gate


def _mid_body(h_ref, ya_ref, yb_ref, wo_ref, p_ref, pn_ref, pu_ref, pg_ref, mn_ref, wi_ref,
              ho_ref, xl_ref, g_ref):
    h = _mix_out(h_ref, ya_ref, yb_ref, wo_ref, p_ref, pn_ref, pu_ref, pg_ref)
    ho_ref[...] = h
    xn = _rms(h, mn_ref[...]).astype(BF16)
    xl_ref[...] = _dg(xn, wi_ref[:, :LRU_W], ((1,), (0,)))
    g_ref[...] = _dg(xn, wi_ref[:, LRU_W:], ((1,), (0,)))


def _final_body(h_ref, ya_ref, yb_ref, wo_ref, p_ref, pn_ref, pu_ref, pg_ref, fn_ref, o_ref):
    h = _mix_out(h_ref, ya_ref, yb_ref, wo_ref, p_ref, pn_ref, pu_ref, pg_ref)
    o_ref[...] = _rms(h, fn_ref[...])


def _row_spec(w, half=None):
    if half is None:
        return pl.BlockSpec((TM, w), lambda i: (i, 0))
    return pl.BlockSpec((TM, w), lambda i: (i, half))


def _mid(h, ya, yb, wo, p, pn, pu, pg, mn, wi):
    t = h.shape[0]
    return pl.pallas_call(
        _mid_body,
        grid=(t // TM,),
        in_specs=[_row_spec(D), _row_spec(D), _row_spec(D), _const_spec(wo.shape),
                  _row_spec(PLE), _const_spec(pn.shape), _const_spec(pu.shape),
                  _const_spec(pg.shape), _const_spec(mn.shape), _const_spec(wi.shape)],
        out_specs=[_row_spec(D), _row_spec(LRU_W), _row_spec(LRU_W)],
        out_shape=[jax.ShapeDtypeStruct((t, D), F32), jax.ShapeDtypeStruct((t, LRU_W), F32),
                   jax.ShapeDtypeStruct((t, LRU_W), F32)],
        compiler_params=_params(("parallel",)),
        name="mid",
    )(h, ya, yb, wo, p, pn, pu, pg, mn, wi)


def _final(h, y, wo, p, pn, pu, pg, fn):
    t = h.shape[0]
    return pl.pallas_call(
        _final_body,
        grid=(t // TM,),
        in_specs=[_row_spec(D), _row_spec(D, 0), _row_spec(D, 1), _const_spec(wo.shape),
                  _row_spec(PLE), _const_spec(pn.shape), _const_spec(pu.shape),
                  _const_spec(pg.shape), _const_spec(fn.shape)],
        out_specs=_row_spec(D),
        out_shape=jax.ShapeDtypeStruct((t, D), F32),
        compiler_params=_params(("parallel",)),
        name="final",
    )(h, y, y, wo, p, pn, pu, pg, fn)


def _rglru_body(xl_ref, g_ref, cw_ref, cb_ref, wr_ref, br_ref, wi_ref, bi_ref, lam_ref, o_ref,
                prev8, hc):
    n = CHUNK
    w = LRU_STRIP

    @pl.when(pl.program_id(2) == 0)
    def _():
        prev8[...] = jnp.zeros_like(prev8)
        hc[...] = jnp.zeros_like(hc)

    row8 = _iota((n // 8, 8, w), 1)

    def chunk_body(c, _):
        rows = pl.ds(pl.multiple_of(c * n, n), n)
        xl = xl_ref[rows, :]
        xc = _conv4(xl, prev8[...], cw_ref, cb_ref)
        prev8[...] = xl[n - 8:, :]
        rg = []
        ig = []
        for j in range(w // LRU_BLK):
            xb = xc[:, j * LRU_BLK:(j + 1) * LRU_BLK].astype(BF16)
            rg.append(_dg(xb, wr_ref[j], ((1,), (0,))))
            ig.append(_dg(xb, wi_ref[j], ((1,), (0,))))
        rg = _sigmoid(jnp.concatenate(rg, axis=1) + br_ref[...])
        ig = _sigmoid(jnp.concatenate(ig, axis=1) + bi_ref[...])
        log_a = -LRU_C * rg * _softplus(-lam_ref[...])
        a = jnp.exp(log_a)
        om = jnp.maximum(1.0 - a * a, 0.0)
        mult = om * lax.rsqrt(jnp.maximum(om, SQRT_FLOOR))
        u = xc * ig * mult
        u = u.reshape(n // 8, 8, w)
        a = a.reshape(n // 8, 8, w)
        for d in (1, 2, 4):
            keep = row8 >= d
            u = u + a * jnp.where(keep, pltpu.roll(u, d, 1), 0.0)
            a = a * jnp.where(keep, pltpu.roll(a, d, 1), 1.0)
        carry = hc[0:1, :]
        hs = []
        for j in range(n // 8):
            hj = u[j] + a[j] * carry
            carry = hj[7:8, :]
            hs.append(hj)
        hc[0:1, :] = carry
        h = jnp.concatenate(hs, axis=0)
        o_ref[rows, :] = (h * _silu(g_ref[rows, :])).astype(o_ref.dtype)
        return 0

    lax.fori_loop(0, LRU_TS // n, chunk_body, 0, unroll=LRU_UNROLL)


def _rglru(xl, g, cw, cb, wr, br, wi, bi, lam):
    bsz, s, _ = xl.shape
    ns = LRU_W // LRU_STRIP
    seq = pl.BlockSpec((None, LRU_TS, LRU_STRIP), lambda b, j, i: (b, i, j))
    vec = lambda r: pl.BlockSpec((r, LRU_STRIP), lambda b, j, i: (0, j))
    blk = pl.BlockSpec((LRU_STRIP // LRU_BLK, LRU_BLK, LRU_BLK), lambda b, j, i: (j, 0, 0))
    return pl.pallas_call(
        _rglru_body,
        grid=(bsz, ns, s // LRU_TS),
        in_specs=[seq, seq, vec(CONV_W), vec(1), blk, vec(1), blk, vec(1), vec(1)],
        out_specs=seq,
        out_shape=jax.ShapeDtypeStruct((bsz, s, LRU_W), BF16),
        scratch_shapes=[pltpu.VMEM((8, LRU_STRIP), F32), pltpu.VMEM((8, LRU_STRIP), F32)],
        compiler_params=_params(("parallel", "parallel", "arbitrary")),
        name="rglru",
    )(xl, g, cw, cb, wr, br, wi, bi, lam)


def _block4_tiles(w):
    nblk = w.shape[0]
    per = ML_HD // QKV_BLOCK
    eye = jnp.eye(per, dtype=w.dtype)
    wt = w.reshape(nblk // per, per, QKV_BLOCK, QKV_BLOCK)
    dense = jnp.einsum("tgio,gh->tgiho", wt, eye)
    return dense.reshape(nblk // per, ML_HD, ML_HD)


def kernel(x, p, mix_norm, pe_norm, final_norm, pe_up, pe_gate, ab_w_in, rwkv_mu, rwkv_mu_lora, rwkv_w0, rwkv_w_up, rwkv_a0, rwkv_a_up, rwkv_k_k, rwkv_k_a, rwkv_r_k, rwkv_ln_w, rwkv_ln_b, mlstm_conv_w, mlstm_conv_b, mlstm_wq, mlstm_wk, mlstm_wv, mlstm_w_if, mlstm_b_if, mlstm_norm, mlstm_skip, ab_w_out, c_w_in, c_conv_w, c_conv_b, c_wr, c_br, c_wi, c_bi, c_lambda, c_w_out):
    bsz, s, _ = x.shape
    t = bsz * s
    row = lambda a: a.reshape(1, -1)
    seq = lambda a: a.reshape(bsz, s, a.shape[-1])

    zr, zk, zv, xm, gr, gm, zl = _proj0(x.reshape(t, D), row(mix_norm[0]),
                                        ab_w_in[0].astype(BF16))

    pr = jnp.concatenate([rwkv_mu[0], row(rwkv_w0[0]), row(rwkv_a0[0]), row(rwkv_k_k[0]),
                          row(rwkv_k_a[0]), row(rwkv_r_k[0]), row(rwkv_ln_w[0]),
                          row(rwkv_ln_b[0]), jnp.zeros((6, D), F32)], axis=0)
    mul = rwkv_mu_lora[0].reshape(1, 128)
    zeros = jnp.zeros((64, D), F32)
    wl = jnp.concatenate([jnp.concatenate([rwkv_w_up[0], zeros], axis=1),
                          jnp.concatenate([zeros, rwkv_a_up[0]], axis=1)], axis=0)
    y_rwkv = _rwkv(seq(zr), seq(zk), seq(zv), seq(zl), seq(gr), pr, mul, wl.astype(BF16))

    wif = mlstm_w_if[0]
    y_m = _mlstm(seq(xm), seq(gm), mlstm_conv_w[0], row(mlstm_conv_b[0]),
                 _block4_tiles(mlstm_wq[0]).astype(BF16), _block4_tiles(mlstm_wk[0]).astype(BF16),
                 _block4_tiles(mlstm_wv[0]).astype(BF16), wif.astype(BF16),
                 wif.T.astype(BF16),
                 row(mlstm_b_if[0]), mlstm_b_if[0].reshape(-1, 1),
                 row(mlstm_norm[0]), row(mlstm_skip[0]))

    h1, xl, g1 = _mid(x.reshape(t, D), y_rwkv.reshape(t, D), y_m.reshape(t, D),
                      ab_w_out[0].astype(BF16), p[0].reshape(t, PLE), row(pe_norm[0]),
                      pe_up[0].astype(BF16), pe_gate[0].astype(BF16), row(mix_norm[1]),
                      c_w_in[0].astype(BF16))

    y_lru = _rglru(seq(xl), seq(g1), c_conv_w[0], row(c_conv_b[0]), c_wr[0].astype(BF16),
                   row(c_br[0]), c_wi[0].astype(BF16), row(c_bi[0]), row(c_lambda[0]))
    out = _final(h1, y_lru.reshape(t, LRU_W), c_w_out[0].astype(BF16), p[1].reshape(t, PLE),
                 row(pe_norm[1]), pe_up[1].astype(BF16), pe_gate[1].astype(BF16),
                 row(final_norm))
    return out.reshape(bsz, s, D)
```

```python
import jax
import jax.numpy as jnp
from jax import lax
from jax.experimental import pallas as pl
from jax.experimental.pallas import tpu as pltpu

F32 = jnp.float32
BF16 = jnp.bfloat16

D = 1024
LANES = 128
CHUNK = 64
NORM_EPS = 1e-6
PLE = 256
RW_HEADS = 16
RW_HD = 64
RW_PAIRS = RW_HEADS // 2
DECAY_SCALE = 0.6065306597126334
RW_GN_EPS = 64e-5
ML_HEADS = 4
ML_HD = 256
ML_EPS = 1e-6
ML_LN_EPS = 1e-5
QKV_BLOCK = 4
CONV_W = 4
LRU_W = 2 * D
LRU_BLK = 128
LRU_C = 8.0
SQRT_FLOOR = 1e-30
AB_IN = 4 * D + 128 + 2 * D

TM = 512
TM_FINAL = 1024
SLAB = 256
SLAB_SKEW = 1
TS = 512
LRU_TS = 1024
LRU_STRIP = 1024
RW_UNROLL = 4
RW_GROUPS = 1
RW_SKEW = 2
ML_UNROLL = 8
ML_SKEW = 2
LRU_UNROLL = 4
VMEM_LIMIT = 56 * 1024 * 1024

assert CHUNK == RW_HD and ML_SKEW >= 2 and RW_GROUPS * RW_SKEW >= 1


def _dg(a, b, dims):
    return lax.dot_general(a, b, (dims, ((), ())), preferred_element_type=F32)


def _split2(x):
    hi = x.astype(BF16)
    lo = (x - hi.astype(F32)).astype(BF16)
    return hi, lo


def _split3(x):
    hi = x.astype(BF16)
    r = x - hi.astype(F32)
    mid = r.astype(BF16)
    lo = (r - mid.astype(F32)).astype(BF16)
    return hi, mid, lo


def _cumsum_rows(tri_bf16, x):
    hi, lo = _split2(x)
    return _dg(tri_bf16, hi, ((1,), (0,))) + _dg(tri_bf16, lo, ((1,), (0,)))


def _mm_exact_lhs(a_bf16, b, dims=((1,), (0,))):
    h, m, l = _split3(b)
    return _dg(a_bf16, h, dims) + (_dg(a_bf16, m, dims) + _dg(a_bf16, l, dims))


def _mm_exact_rhs(a, b_bf16, dims=((1,), (0,))):
    h, m, l = _split3(a)
    return _dg(h, b_bf16, dims) + (_dg(m, b_bf16, dims) + _dg(l, b_bf16, dims))


def _rms(x, g):
    return x * lax.rsqrt(jnp.mean(x * x, axis=-1, keepdims=True) + NORM_EPS) * g


def _sigmoid(x):
    return jax.nn.sigmoid(x)


def _silu(x):
    return x * _sigmoid(x)


def _softplus(x):
    return jnp.maximum(x, 0.0) + jnp.log(1.0 + jnp.exp(-jnp.abs(x)))


def _iota(shape, axis):
    return lax.broadcasted_iota(jnp.int32, shape, axis)


def _tri_incl(n):
    return jnp.where(_iota((n, n), 1) <= _iota((n, n), 0), 1.0, 0.0).astype(BF16)


def _shift_rows(x, d, prev8):
    xs = pltpu.roll(x, d, 0)
    ps = pltpu.roll(prev8, d, 0)
    head = jnp.where(_iota((8, x.shape[1]), 0) < d, ps, xs[:8])
    return jnp.concatenate([head, xs[8:]], axis=0)


def _conv4(x, prev8, w_ref, b_ref):
    acc = x * w_ref[3:4, :] + b_ref[...]
    for d in (1, 2, 3):
        acc = acc + _shift_rows(x, d, prev8) * w_ref[3 - d:4 - d, :]
    return acc


def _run_skewed(tasks, skew):
    pending = list(tasks)
    active = []
    tick = 0
    while pending or active:
        if pending and tick % skew == 0:
            active.append(pending.pop(0))
        for t in list(active):
            try:
                next(t)
            except StopIteration:
                active.remove(t)
        tick += 1


def _slabs(tm):
    return [slice(j * SLAB, (j + 1) * SLAB) for j in range(tm // SLAB)]


def _const_spec(shape):
    nd = len(shape)
    return pl.BlockSpec(shape, lambda *_: (0,) * nd, pipeline_mode=pl.Buffered(1))


def _params(sem):
    return pltpu.CompilerParams(dimension_semantics=sem, vmem_limit_bytes=VMEM_LIMIT)


def _proj0_body(x_ref, g_ref, w_ref, zr, zk, zv, xm, gr, gm, zl):
    nn = ((1,), (0,))

    def task(rows):
        xn = _rms(x_ref[rows, :], g_ref[...]).astype(BF16)
        yield
        for j, o in enumerate((zr, zk, zv)):
            o[rows, :] = _dg(xn, w_ref[:, j * D:(j + 1) * D], nn)
        zl[rows, :] = _dg(xn, w_ref[:, 3 * D:3 * D + 128], nn)
        yield
        for j, o in enumerate((xm, gr, gm)):
            o[rows, :] = _dg(xn, w_ref[:, 3 * D + 128 + j * D:3 * D + 128 + (j + 1) * D], nn)

    _run_skewed([task(r) for r in _slabs(TM)], SLAB_SKEW)


def _proj0(x2, g, w):
    t = x2.shape[0]
    wide = jax.ShapeDtypeStruct((t, D), F32)
    row = pl.BlockSpec((TM, D), lambda i: (i, 0))
    return pl.pallas_call(
        _proj0_body,
        grid=(t // TM,),
        in_specs=[row, _const_spec((1, D)), _const_spec((D, AB_IN))],
        out_specs=[row] * 6 + [pl.BlockSpec((TM, 128), lambda i: (i, 0))],
        out_shape=[wide] * 6 + [jax.ShapeDtypeStruct((t, 128), F32)],
        compiler_params=_params(("parallel",)),
        name="proj0",
    )(x2, g, w)


_P_MUR, _P_MUK, _P_MUV, _P_W0, _P_A0, _P_KK, _P_KA, _P_RK, _P_LNW, _P_LNB = range(10)


def _rwkv_task(get_ops, pairs, state, masks, emit):
    m_top, rowhead, strict, incl, eye = masks
    n = CHUNK
    rng = range(len(pairs))
    bf = lambda t: t.astype(BF16)
    nn, nt, tn = ((1,), (0,)), ((1,), (1,)), ((0,), (0,))

    def stack(x):
        z = jnp.zeros_like(x)
        return jnp.concatenate([jnp.where(m_top, x, z), jnp.where(m_top, z, x)], axis=0)

    def unstack_col(c):
        return jnp.where(m_top, c[:n], c[n:])

    def head_sum(x):
        return unstack_col(jnp.sum(stack(x), axis=-1, keepdims=True))

    pre = []
    for r, ke, v, lw, g, a, kkraw, rk, lnw, lnb in get_ops():
        kk = kkraw / jnp.maximum(jnp.sqrt(head_sum(jnp.square(kkraw))), 1e-12)
        bonus = head_sum(r * ke * rk)
        g_last = g[n - 1:n, :]
        b = kk * a
        e_neg = jnp.exp(-g)
        e_tail = jnp.exp(g_last - g)
        ag = bf(-kk * jnp.exp(g - lw))
        vb = bf(v)
        pre.append(dict(
            bonus=bonus, v=v, lnw=lnw, lnb=lnb, vb=vb, ag=ag, rg=bf(r * jnp.exp(g)),
            bk_s=jnp.concatenate([stack(bf(b * e_neg)), stack(bf(ke * e_neg))], axis=0),
            bkd=jnp.concatenate([bf(b * e_tail), bf(ke * e_tail)], axis=0),
            e_last_col=jnp.transpose(jnp.broadcast_to(jnp.exp(g_last), (LANES, LANES)))))
    yield

    sc = [_dg(jnp.concatenate([q["ag"], q["rg"]], axis=0), q["bk_s"], nt) for q in pre]
    n_ab_f = [jnp.where(strict, t[:n, :LANES], 0.0) for t in sc]
    n_ab = [bf(t) for t in n_ab_f]
    nr_k = [bf(jnp.concatenate([jnp.where(strict, t[:n, LANES:], 0.0),
                                jnp.where(incl, t[n:, LANES:], 0.0)], axis=0)) for t in sc]
    a_rb = [bf(jnp.where(incl, t[n:, :LANES], 0.0)) for t in sc]
    yield

    kv = [_dg(nr_k[i], stack(pre[i]["vb"]), nn) for i in rng]
    yield

    pk = n_ab
    t = [eye + n_ab_f[i] for i in rng]
    p_next = [bf(_dg(pk[i], stack(pk[i]), nn)) for i in rng]
    yield
    for step in range(1, 6):
        pk = p_next
        if step < 5:
            m = [_dg(pk[i], jnp.concatenate([stack(bf(t[i])), stack(pk[i])], axis=1), nn)
                 for i in rng]
            p_next = [bf(m[i][:, LANES:]) for i in rng]
            t = [t[i] + m[i][:, :LANES] for i in rng]
        else:
            t = [t[i] + _dg(pk[i], stack(bf(t[i])), nn) for i in rng]
        yield

    x = [_dg(bf(t[i]), jnp.concatenate([stack(pre[i]["ag"]), stack(bf(kv[i][:n]))], axis=1), nn)
         for i in rng]
    yield

    sts = [state[p] for p in pairs]
    stb = [bf(s_) for s_ in sts]
    u = [_dg(bf(x[i][:, :LANES]), stb[i], nn) + x[i][:, LANES:] for i in rng]
    ub = [bf(u_) for u_ in u]
    yield

    for i, p in enumerate(pairs):
        upd = _dg(pre[i]["bkd"], jnp.concatenate([ub[i], pre[i]["vb"]], axis=0), tn)
        state[p] = pre[i]["e_last_col"] * sts[i] + jnp.where(rowhead, upd, 0.0)
    y = [_dg(jnp.concatenate([pre[i]["rg"], a_rb[i]], axis=1),
             jnp.concatenate([stb[i], stack(ub[i])], axis=0), nn) + kv[i][n:] for i in rng]
    yield

    for i, p in enumerate(pairs):
        q = pre[i]
        dlt = y[i] - head_sum(y[i]) * (1.0 / RW_HD)
        var = head_sum(dlt * dlt) * (1.0 / RW_HD)
        yn = dlt * lax.rsqrt(var + RW_GN_EPS)
        emit(p, yn * q["lnw"] + q["lnb"] + q["bonus"] * q["v"])


def _rwkv_body(zr_ref, zk_ref, zv_ref, zl_ref, gr_ref, pr_ref, mul_ref, wl_ref, o_ref,
               carry, st_ref):
    n = CHUNK

    @pl.when(pl.program_id(1) == 0)
    def _():
        carry[...] = jnp.zeros_like(carry)
        st_ref[...] = jnp.zeros_like(st_ref)

    lane = _iota((1, LANES), 1)
    m_top = lane < RW_HD
    r2 = _iota((2 * n, LANES), 0)
    c2 = _iota((2 * n, LANES), 1)
    rowhead = (r2 < n) == (c2 < RW_HD)
    rw = _iota((n, LANES), 0)
    sw = jnp.bitwise_and(_iota((n, LANES), 1), RW_HD - 1)
    strict = sw < rw
    incl = sw <= rw
    eye = jnp.where(sw == rw, 1.0, 0.0)
    masks = (m_top, rowhead, strict, incl, eye)
    tri = _tri_incl(n)
    first_row = _iota((n, 1), 0) == 0

    def shift_mix(z, prev_row, mu):
        prev = jnp.where(first_row, prev_row, pltpu.roll(z, 1, 0))
        return z + (prev - z) * mu

    def body(it, _):
        state = {p: st_ref[p] for p in range(RW_PAIRS)}
        last = dict(r=carry[0:1, 0:D], k=carry[0:1, D:2 * D], v=carry[0:1, 2 * D:3 * D],
                    l=carry[0:1, 3 * D:3 * D + LANES])
        chunk_vals = {}

        def chunk_level(j):
            if j in chunk_vals:
                return chunk_vals[j]
            rows = pl.ds(pl.multiple_of((it * RW_UNROLL + j) * n, n), n)
            zr = zr_ref[rows, :]
            zk = zk_ref[rows, :]
            zv = zv_ref[rows, :]
            zl = zl_ref[rows, :]
            r = shift_mix(zr, last["r"], pr_ref[_P_MUR:_P_MUR + 1, :])
            k = shift_mix(zk, last["k"], pr_ref[_P_MUK:_P_MUK + 1, :])
            v = shift_mix(zv, last["v"], pr_ref[_P_MUV:_P_MUV + 1, :])
            lo = shift_mix(zl, last["l"], mul_ref[...])
            last.update(r=zr[n - 1:n, :], k=zk[n - 1:n, :], v=zv[n - 1:n, :], l=zl[n - 1:n, :])
            lo = jnp.where(_iota((n, LANES), 1) < 64, jnp.tanh(lo), lo)
            up = _dg(lo.astype(BF16), wl_ref[...], ((1,), (0,)))
            lw = -DECAY_SCALE * _sigmoid(pr_ref[_P_W0:_P_W0 + 1, :] + up[:, :D])
            a = _sigmoid(pr_ref[_P_A0:_P_A0 + 1, :] + up[:, D:])
            g = _cumsum_rows(tri, lw)
            ke = k * (1.0 + (a - 1.0) * pr_ref[_P_KA:_P_KA + 1, :])
            kkraw = k * pr_ref[_P_KK:_P_KK + 1, :]
            chunk_vals[j] = (rows, (r, ke, v, lw, g, a, kkraw))
            return chunk_vals[j]

        def make_task(j, pairs):
            def get_ops():
                _, arrs = chunk_level(j)
                out = []
                for p in pairs:
                    ln = slice(p * LANES, (p + 1) * LANES)
                    out.append(tuple(t[:, ln] for t in arrs) + (
                        pr_ref[_P_RK:_P_RK + 1, ln], pr_ref[_P_LNW:_P_LNW + 1, ln],
                        pr_ref[_P_LNB:_P_LNB + 1, ln]))
                return out

            def emit(p, y):
                rows, _ = chunk_level(j)
                ln = slice(p * LANES, (p + 1) * LANES)
                o_ref[rows, ln] = (y * _silu(gr_ref[rows, ln])).astype(o_ref.dtype)

            return _rwkv_task(get_ops, pairs, state, masks, emit)

        per_group = RW_PAIRS // RW_GROUPS
        groups = [list(range(gi * per_group, (gi + 1) * per_group)) for gi in range(RW_GROUPS)]
        _run_skewed([make_task(j, grp) for j in range(RW_UNROLL) for grp in groups], RW_SKEW)

        for p in range(RW_PAIRS):
            st_ref[p] = state[p]
        carry[0:1, 0:D] = last["r"]
        carry[0:1, D:2 * D] = last["k"]
        carry[0:1, 2 * D:3 * D] = last["v"]
        carry[0:1, 3 * D:3 * D + LANES] = last["l"]
        return 0

    lax.fori_loop(0, TS // (n * RW_UNROLL), body, 0)


def _rwkv(zr, zk, zv, zl, gr, pr, mul, wl):
    bsz, s, _ = zr.shape
    seq = lambda w: pl.BlockSpec((None, TS, w), lambda b, i: (b, i, 0))
    return pl.pallas_call(
        _rwkv_body,
        grid=(bsz, s // TS),
        in_specs=[seq(D), seq(D), seq(D), seq(LANES), seq(D),
                  _const_spec(pr.shape), _const_spec(mul.shape), _const_spec(wl.shape)],
        out_specs=seq(D),
        out_shape=jax.ShapeDtypeStruct((bsz, s, D), BF16),
        scratch_shapes=[pltpu.VMEM((8, 3 * D + LANES), F32),
                        pltpu.VMEM((RW_PAIRS, LANES, LANES), F32)],
        compiler_params=_params(("parallel", "arbitrary")),
        name="rwkv",
    )(zr, zk, zv, zl, gr, pr, mul, wl)


def _log_sigmoid(x):
    return jnp.minimum(x, 0.0) - jnp.log(1.0 + jnp.exp(-jnp.abs(x)))


def _mlstm_task(rows, refs, state, consts):
    (xm_ref, gm_ref, cw_ref, cb_ref, wq_ref, wk_ref, wv_ref, wif_ref, wift_ref, bif_ref,
     bift_ref, nrm_ref, skp_ref, o_ref) = refs
    tri, causal = consts
    n = CHUNK
    nn, nt, tn = ((1,), (0,)), ((1,), (1,)), ((0,), (0,))
    hs = range(ML_HEADS)
    lns = [slice(h * ML_HD, (h + 1) * ML_HD) for h in hs]
    k_scale = ML_HD ** -0.5

    xm = xm_ref[rows, :]
    xc = _silu(_conv4(xm, state["prev8"], cw_ref, cb_ref))
    state["prev8"] = xm[n - 8:, :]
    yield

    xcb = xc.astype(BF16)
    xmb = xm.astype(BF16)
    qb = [_dg(xcb[:, ln], wq_ref[h], nn).astype(BF16) for h, ln in enumerate(lns)]
    kf = [_dg(xcb[:, ln], wk_ref[h], nn) for h, ln in enumerate(lns)]
    vf = [_dg(xmb[:, ln], wv_ref[h], nn) for h, ln in enumerate(lns)]
    kb = [t.astype(BF16) for t in kf]
    vb = [t.astype(BF16) for t in vf]
    yield

    qkv = jnp.concatenate(qb + kb + vb, axis=1)
    gc = bif_ref[...] + _dg(qkv, wif_ref[...], nn)
    gt = bift_ref[...] + _dg(wift_ref[...], qkv, nt)
    bc_c = _mm_exact_lhs(tri, _log_sigmoid(gc))
    bc_t = _mm_exact_rhs(_log_sigmoid(gt), tri, nt)
    li_c = [gc[:, h:h + 1] for h in hs]
    li_t = [gt[h:h + 1, :] for h in hs]
    b_c = [bc_c[:, ML_HEADS + h:ML_HEADS + h + 1] for h in hs]
    b_t = [bc_t[ML_HEADS + h:ML_HEADS + h + 1, :] for h in hs]
    qk = [_dg(qb[h], kb[h], nt) for h in hs]
    log_d = [jnp.where(causal, b_c[h] - b_t[h] + li_t[h], -jnp.inf) for h in hs]
    b_last = [b_c[h][n - 1:n, :] for h in hs]
    log_g = [b_last[h] - b_c[h] + li_c[h] for h in hs]
    yield

    m_prev = [state["m", h] for h in hs]
    ct = [state["ct", h] for h in hs]
    nv = [state["nv", h] for h in hs]
    qc = [_dg(qb[h], ct[h].astype(BF16), nn) for h in hs]
    log_inter = [b_c[h] + m_prev[h] for h in hs]
    m_t = [jnp.maximum(jnp.max(log_d[h], axis=-1, keepdims=True), log_inter[h]) for h in hs]
    s = [qk[h] * k_scale * jnp.exp(log_d[h] - m_t[h]) for h in hs]
    w_inter = [jnp.exp(log_inter[h] - m_t[h]) for h in hs]
    m_new = [jnp.maximum(b_last[h] + m_prev[h], jnp.max(log_g[h], axis=0, keepdims=True))
             for h in hs]
    gcol = [jnp.exp(log_g[h] - m_new[h]) for h in hs]
    decay = [jnp.exp(b_last[h] + m_prev[h] - m_new[h]) for h in hs]
    yield

    sv = [_dg(s[h].astype(BF16), vb[h], nn) for h in hs]
    kgv = [_dg(kb[h], (gcol[h] * vf[h]).astype(BF16), tn) for h in hs]
    yield

    for h in hs:
        state["ct", h] = decay[h] * ct[h] + kgv[h] * k_scale
        state["nv", h] = decay[h] * nv[h] + k_scale * jnp.sum(gcol[h] * kf[h], axis=0,
                                                              keepdims=True)
        state["m", h] = m_new[h]
    yield

    for h in hs:
        ln = lns[h]
        num = sv[h] + w_inter[h] * qc[h]
        den = (jnp.sum(s[h], axis=-1, keepdims=True)
               + w_inter[h] * jnp.sum(qb[h].astype(F32) * nv[h], axis=-1, keepdims=True))
        hh = num / (jnp.maximum(jnp.abs(den), jnp.exp(-m_t[h])) + ML_EPS)
        mu = jnp.mean(hh, axis=-1, keepdims=True)
        dlt = hh - mu
        var = jnp.mean(dlt * dlt, axis=-1, keepdims=True)
        hn = dlt * lax.rsqrt(var + ML_LN_EPS) * nrm_ref[:, ln]
        ym = hn + skp_ref[:, ln] * xc[:, ln]
        o_ref[rows, ln] = (ym * _silu(gm_ref[rows, ln])).astype(o_ref.dtype)


def _mlstm_body(xm_ref, gm_ref, cw_ref, cb_ref, wq_ref, wk_ref, wv_ref, wif_ref, wift_ref,
                bif_ref, bift_ref, nrm_ref, skp_ref, o_ref, prev8, ct_ref, n_ref, m_ref):
    n = CHUNK

    @pl.when(pl.program_id(1) == 0)
    def _():
        prev8[...] = jnp.zeros_like(prev8)
        ct_ref[...] = jnp.zeros_like(ct_ref)
        n_ref[...] = jnp.zeros_like(n_ref)
        m_ref[...] = jnp.zeros_like(m_ref)

    consts = (_tri_incl(n), _iota((n, n), 1) <= _iota((n, n), 0))
    refs = (xm_ref, gm_ref, cw_ref, cb_ref, wq_ref, wk_ref, wv_ref, wif_ref, wift_ref, bif_ref,
            bift_ref, nrm_ref, skp_ref, o_ref)

    def body(it, _):
        state = {"prev8": prev8[...]}
        for h in range(ML_HEADS):
            state["ct", h] = ct_ref[h]
            state["nv", h] = n_ref[0:1, h * ML_HD:(h + 1) * ML_HD]
            state["m", h] = m_ref[h:h + 1, 0:1]
        tasks = [_mlstm_task(pl.ds(pl.multiple_of((it * ML_UNROLL + j) * n, n), n), refs, state,
                             consts) for j in range(ML_UNROLL)]
        _run_skewed(tasks, ML_SKEW)
        prev8[...] = state["prev8"]
        for h in range(ML_HEADS):
            ct_ref[h] = state["ct", h]
            n_ref[0:1, h * ML_HD:(h + 1) * ML_HD] = state["nv", h]
            m_ref[h:h + 1, :] = jnp.broadcast_to(state["m", h], (1, LANES))
        return 0

    lax.fori_loop(0, TS // (n * ML_UNROLL), body, 0)


def _mlstm(xm, gm, cw, cb, wq, wk, wv, wif, wift, bif, bift, nrm, skp):
    bsz, s, _ = xm.shape
    seq = pl.BlockSpec((None, TS, D), lambda b, i: (b, i, 0))
    consts = (cw, cb, wq, wk, wv, wif, wift, bif, bift, nrm, skp)
    return pl.pallas_call(
        _mlstm_body,
        grid=(bsz, s // TS),
        in_specs=[seq, seq] + [_const_spec(a.shape) for a in consts],
        out_specs=seq,
        out_shape=jax.ShapeDtypeStruct((bsz, s, D), BF16),
        scratch_shapes=[pltpu.VMEM((8, D), F32),
                        pltpu.VMEM((ML_HEADS, ML_HD, ML_HD), F32),
                        pltpu.VMEM((8, D), F32),
                        pltpu.VMEM((8, LANES), F32)],
        compiler_params=_params(("parallel", "arbitrary")),
        name="mlstm",
    )(xm, gm, *consts)


def _mix_task(rows, refs, tail):
    h_ref, ya_ref, yb_ref, wo_ref, p_ref, pn_ref, pu_ref, pg_ref = refs
    nn = ((1,), (0,))
    h = h_ref[rows, :] + _dg(ya_ref[rows, :], wo_ref[:D, :], nn)
    h = h + _dg(yb_ref[rows, :], wo_ref[D:, :], nn)
    yield
    hn = _rms(h, pn_ref[...]).astype(BF16)
    yield
    gate = _dg(hn, pg_ref[...], nn)
    pe = _dg(p_ref[rows, :].astype(BF16), pu_ref[...], nn)
    yield
    h = h + pe * _sigmoid(gate)
    yield from tail(rows, h)


def _mid_body(h_ref, ya_ref, yb_ref, wo_ref, p_ref, pn_ref, pu_ref, pg_ref, mn_ref, wi_ref,
              ho_ref, xl_ref, g_ref):
    refs = (h_ref, ya_ref, yb_ref, wo_ref, p_ref, pn_ref, pu_ref, pg_ref)

    def tail(rows, h):
        ho_ref[rows, :] = h
        xn = _rms(h, mn_ref[...]).astype(BF16)
        yield
        xl_ref[rows, :] = _dg(xn, wi_ref[:, :LRU_W], ((1,), (0,)))
        g_ref[rows, :] = _dg(xn, wi_ref[:, LRU_W:], ((1,), (0,)))

    _run_skewed([_mix_task(r, refs, tail) for r in _slabs(TM)], SLAB_SKEW)


def _final_body(h_ref, ya_ref, yb_ref, wo_ref, p_ref, pn_ref, pu_ref, pg_ref, fn_ref, o_ref):
    refs = (h_ref, ya_ref, yb_ref, wo_ref, p_ref, pn_ref, pu_ref, pg_ref)

    def tail(rows, h):
        o_ref[rows, :] = _rms(h, fn_ref[...])
        yield

    _run_skewed([_mix_task(r, refs, tail) for r in _slabs(TM_FINAL)], SLAB_SKEW)


def _row_spec(w, half=None, tm=None):
    tm = TM if tm is None else tm
    if half is None:
        return pl.BlockSpec((tm, w), lambda i: (i, 0))
    return pl.BlockSpec((tm, w), lambda i: (i, half))


def _mid(h, ya, yb, wo, p, pn, pu, pg, mn, wi):
    t = h.shape[0]
    return pl.pallas_call(
        _mid_body,
        grid=(t // TM,),
        in_specs=[_row_spec(D), _row_spec(D), _row_spec(D), _const_spec(wo.shape),
                  _row_spec(PLE), _const_spec(pn.shape), _const_spec(pu.shape),
                  _const_spec(pg.shape), _const_spec(mn.shape), _const_spec(wi.shape)],
        out_specs=[_row_spec(D), _row_spec(LRU_W), _row_spec(LRU_W)],
        out_shape=[jax.ShapeDtypeStruct((t, D), F32), jax.ShapeDtypeStruct((t, LRU_W), F32),
                   jax.ShapeDtypeStruct((t, LRU_W), F32)],
        compiler_params=_params(("parallel",)),
        name="mid",
    )(h, ya, yb, wo, p, pn, pu, pg, mn, wi)


def _final(h, y, wo, p, pn, pu, pg, fn):
    t = h.shape[0]
    rs = lambda w, half=None: _row_spec(w, half, TM_FINAL)
    return pl.pallas_call(
        _final_body,
        grid=(t // TM_FINAL,),
        in_specs=[rs(D), rs(D, 0), rs(D, 1), _const_spec(wo.shape),
                  rs(PLE), _const_spec(pn.shape), _const_spec(pu.shape),
                  _const_spec(pg.shape), _const_spec(fn.shape)],
        out_specs=rs(D),
        out_shape=jax.ShapeDtypeStruct((t, D), F32),
        compiler_params=_params(("parallel",)),
        name="final",
    )(h, y, y, wo, p, pn, pu, pg, fn)


def _rglru_body(xl_ref, g_ref, cw_ref, cb_ref, wr_ref, br_ref, wi_ref, bi_ref, lam_ref, o_ref,
                prev8, hc):
    n = CHUNK
    w = LRU_STRIP

    @pl.when(pl.program_id(2) == 0)
    def _():
        prev8[...] = jnp.zeros_like(prev8)
        hc[...] = jnp.zeros_like(hc)

    row8 = _iota((n // 8, 8, w), 1)

    def chunk_body(c, _):
        rows = pl.ds(pl.multiple_of(c * n, n), n)
        xl = xl_ref[rows, :]
        xc = _conv4(xl, prev8[...], cw_ref, cb_ref)
        prev8[...] = xl[n - 8:, :]
        rg = []
        ig = []
        for j in range(w // LRU_BLK):
            xb = xc[:, j * LRU_BLK:(j + 1) * LRU_BLK].astype(BF16)
            rg.append(_dg(xb, wr_ref[j], ((1,), (0,))))
            ig.append(_dg(xb, wi_ref[j], ((1,), (0,))))
        rg = _sigmoid(jnp.concatenate(rg, axis=1) + br_ref[...])
        ig = _sigmoid(jnp.concatenate(ig, axis=1) + bi_ref[...])
        log_a = -LRU_C * rg * _softplus(-lam_ref[...])
        a = jnp.exp(log_a)
        om = jnp.maximum(1.0 - a * a, 0.0)
        mult = om * lax.rsqrt(jnp.maximum(om, SQRT_FLOOR))
        u = xc * ig * mult
        u = u.reshape(n // 8, 8, w)
        a = a.reshape(n // 8, 8, w)
        for d in (1, 2, 4):
            keep = row8 >= d
            u = u + a * jnp.where(keep, pltpu.roll(u, d, 1), 0.0)
            a = a * jnp.where(keep, pltpu.roll(a, d, 1), 1.0)
        carry = hc[0:1, :]
        hs = []
        for j in range(n // 8):
            hj = u[j] + a[j] * carry
            carry = hj[7:8, :]
            hs.append(hj)
        hc[0:1, :] = carry
        h = jnp.concatenate(hs, axis=0)
        o_ref[rows, :] = (h * _silu(g_ref[rows, :])).astype(o_ref.dtype)
        return 0

    lax.fori_loop(0, LRU_TS // n, chunk_body, 0, unroll=LRU_UNROLL)


def _rglru(xl, g, cw, cb, wr, br, wi, bi, lam):
    bsz, s, _ = xl.shape
    ns = LRU_W // LRU_STRIP
    seq = pl.BlockSpec((None, LRU_TS, LRU_STRIP), lambda b, j, i: (b, i, j))
    vec = lambda r: pl.BlockSpec((r, LRU_STRIP), lambda b, j, i: (0, j))
    blk = pl.BlockSpec((LRU_STRIP // LRU_BLK, LRU_BLK, LRU_BLK), lambda b, j, i: (j, 0, 0))
    return pl.pallas_call(
        _rglru_body,
        grid=(bsz, ns, s // LRU_TS),
        in_specs=[seq, seq, vec(CONV_W), vec(1), blk, vec(1), blk, vec(1), vec(1)],
        out_specs=seq,
        out_shape=jax.ShapeDtypeStruct((bsz, s, LRU_W), BF16),
        scratch_shapes=[pltpu.VMEM((8, LRU_STRIP), F32), pltpu.VMEM((8, LRU_STRIP), F32)],
        compiler_params=_params(("parallel", "parallel", "arbitrary")),
        name="rglru",
    )(xl, g, cw, cb, wr, br, wi, bi, lam)


def _block4_tiles(w):
    nblk = w.shape[0]
    per = ML_HD // QKV_BLOCK
    eye = jnp.eye(per, dtype=w.dtype)
    wt = w.reshape(nblk // per, per, QKV_BLOCK, QKV_BLOCK)
    dense = jnp.einsum("tgio,gh->tgiho", wt, eye)
    return dense.reshape(nblk // per, ML_HD, ML_HD)


def kernel(x, p, mix_norm, pe_norm, final_norm, pe_up, pe_gate, ab_w_in, rwkv_mu, rwkv_mu_lora, rwkv_w0, rwkv_w_up, rwkv_a0, rwkv_a_up, rwkv_k_k, rwkv_k_a, rwkv_r_k, rwkv_ln_w, rwkv_ln_b, mlstm_conv_w, mlstm_conv_b, mlstm_wq, mlstm_wk, mlstm_wv, mlstm_w_if, mlstm_b_if, mlstm_norm, mlstm_skip, ab_w_out, c_w_in, c_conv_w, c_conv_b, c_wr, c_br, c_wi, c_bi, c_lambda, c_w_out):
    bsz, s, _ = x.shape
    t = bsz * s
    row = lambda a: a.reshape(1, -1)
    seq = lambda a: a.reshape(bsz, s, a.shape[-1])

    zr, zk, zv, xm, gr, gm, zl = _proj0(x.reshape(t, D), row(mix_norm[0]),
                                        ab_w_in[0].astype(BF16))

    pr = jnp.concatenate([rwkv_mu[0], row(rwkv_w0[0]), row(rwkv_a0[0]), row(rwkv_k_k[0]),
                          row(rwkv_k_a[0]), row(rwkv_r_k[0]), row(rwkv_ln_w[0]),
                          row(rwkv_ln_b[0]), jnp.zeros((6, D), F32)], axis=0)
    mul = rwkv_mu_lora[0].reshape(1, 128)
    zeros = jnp.zeros((64, D), F32)
    wl = jnp.concatenate([jnp.concatenate([rwkv_w_up[0], zeros], axis=1),
                          jnp.concatenate([zeros, rwkv_a_up[0]], axis=1)], axis=0)
    y_rwkv = _rwkv(seq(zr), seq(zk), seq(zv), seq(zl), seq(gr), pr, mul, wl.astype(BF16))

    wif = mlstm_w_if[0]
    y_m = _mlstm(seq(xm), seq(gm), mlstm_conv_w[0], row(mlstm_conv_b[0]),
                 _block4_tiles(mlstm_wq[0]).astype(BF16), _block4_tiles(mlstm_wk[0]).astype(BF16),
                 _block4_tiles(mlstm_wv[0]).astype(BF16), wif.astype(BF16),
                 wif.T.astype(BF16),
                 row(mlstm_b_if[0]), mlstm_b_if[0].reshape(-1, 1),
                 row(mlstm_norm[0]), row(mlstm_skip[0]))

    h1, xl, g1 = _mid(x.reshape(t, D), y_rwkv.reshape(t, D), y_m.reshape(t, D),
                      ab_w_out[0].astype(BF16), p[0].reshape(t, PLE), row(pe_norm[0]),
                      pe_up[0].astype(BF16), pe_gate[0].astype(BF16), row(mix_norm[1]),
                      c_w_in[0].astype(BF16))

    y_lru = _rglru(seq(xl), seq(g1), c_conv_w[0], row(c_conv_b[0]), c_wr[0].astype(BF16),
                   row(c_br[0]), c_wi[0].astype(BF16), row(c_bi[0]), row(c_lambda[0]))
    out = _final(h1, y_lru.reshape(t, LRU_W), c_w_out[0].astype(BF16), p[1].reshape(t, PLE),
                 row(pe_norm[1]), pe_up[1].astype(BF16), pe_gate[1].astype(BF16),
                 row(final_norm))
    return out.reshape(bsz, s, D)
```
